```python
import jax, jax.numpy as jnp
from jax import lax
import numpy as np

D_MODEL = 1024
BATCH = 8
SEQ = 2048
DEPTH = 4
DEC_BATCH = 32
DEC_SEQ = 1
PAST_LEN = 8192
PAGE_SIZE = 128

POOL_WINDOWS = (2, 4, 8, 16)
A_WIDTH = D_MODEL // 2
POOL_GROUP_WIDTH = A_WIDTH // len(POOL_WINDOWS)
POOL_PREFIX = max(POOL_WINDOWS) - 1
HD_B = 64
B_WIDTH = D_MODEL // 2
H_B = B_WIDTH // HD_B
Q_BLOCK = 128
NEG_INF = -1e30
HD_C = 64
C_WIDTH = D_MODEL // 2
H_C = C_WIDTH // HD_C
DECAY_LORA = 64
AAA_LORA = 64
GATE_LORA = 128
C_IN = 3 * C_WIDTH + DECAY_LORA + AAA_LORA + GATE_LORA
LN_X_EPS = 64e-5
N_BRANCH = 3
D_FF = 4 * D_MODEL
RMS_EPS = 1e-6
OFF_A = 0
OFF_Q = OFF_A + A_WIDTH
OFF_K = OFF_Q + B_WIDTH
OFF_V = OFF_K + B_WIDTH
OFF_F = OFF_V + B_WIDTH
OFF_C = OFF_F + H_B
OFF_G = OFF_C + C_IN
IN_COLS = OFF_G + N_BRANCH * D_MODEL

kernel_name = "hybrid_pool_fox_rwkv7_decode_step"


def rms_norm(x, g):
    xf = x.astype(jnp.float32)
    y = xf * lax.rsqrt(jnp.mean(xf * xf, axis=-1, keepdims=True) + RMS_EPS)
    return (y * g.astype(jnp.float32)).astype(x.dtype)


def pool_mix(u, pool_prev, pos0, w_pool, pool_scale):
    B, T, _ = u.shape
    P = POOL_PREFIX
    ext = jnp.concatenate([pool_prev.astype(jnp.float32), u.astype(jnp.float32)], axis=1)
    cs = jnp.concatenate([jnp.zeros((B, 1, A_WIDTH), jnp.float32), jnp.cumsum(ext, axis=1)], axis=1)
    pos = pos0 + jnp.arange(T)
    groups = []
    for g, w in enumerate(POOL_WINDOWS):
        lo_c, hi_c = g * POOL_GROUP_WIDTH, (g + 1) * POOL_GROUP_WIDTH
        win_sum = cs[:, P + 1:P + 1 + T, lo_c:hi_c] - cs[:, P + 1 - w:P + 1 - w + T, lo_c:hi_c]
        cnt = jnp.minimum(w, pos + 1).astype(jnp.float32)[None, :, None]
        groups.append(win_sum / cnt - ext[:, P:, lo_c:hi_c])
    pooled = jnp.stack(groups, axis=2)
    mixed = jnp.einsum('btgc,gcd->btgd', pooled, w_pool.astype(jnp.float32)).reshape(B, T, A_WIDTH)
    y = mixed * pool_scale.astype(jnp.float32)
    return y.astype(u.dtype), ext[:, -P:].astype(u.dtype)


def fox_attention(q, k, v, c_q, c_k, q_pos0):
    B, Tq, H, Dh = q.shape
    Tk = k.shape[1]
    blk = Q_BLOCK if Tq % Q_BLOCK == 0 else Tq
    nb = Tq // blk
    q_blocks = jnp.moveaxis(q.reshape(B, nb, blk, H, Dh), 1, 0)
    c_blocks = jnp.moveaxis(c_q.reshape(B, nb, blk, H), 1, 0)
    ck_t = jnp.swapaxes(c_k, 1, 2)
    key_pos = jnp.arange(Tk)
    scale = HD_B ** -0.5

    def attend_block(args):
        q_i, c_i, i = args
        s = jnp.einsum('bqhd,bkhd->bhqk', q_i, k, preferred_element_type=jnp.float32) * scale
        s = s + jnp.swapaxes(c_i, 1, 2)[..., None] - ck_t[:, :, None, :]
        q_pos = q_pos0 + i * blk + jnp.arange(blk)
        s = jnp.where(key_pos[None, :] <= q_pos[:, None], s, NEG_INF)
        p = jax.nn.softmax(s, axis=-1)
        return jnp.einsum('bhqk,bkhd->bqhd', p.astype(v.dtype), v)

    o = lax.map(attend_block, (q_blocks, c_blocks, jnp.arange(nb)))
    return jnp.moveaxis(o, 0, 1).reshape(B, Tq, H, Dh)


def rwkv7_mix(pc, shift_prev, wkv_prev, mu, w0, w_up, a0, a_up, g_up, k_k, k_a, r_k, ln_w, ln_b):
    B, T, _ = pc.shape
    f32 = jnp.float32
    pcf = pc.astype(f32)
    prev = jnp.concatenate([shift_prev[:, None].astype(f32), pcf[:, :-1]], axis=1)
    xs = pcf + (prev - pcf) * mu.astype(f32)
    r = xs[..., 0:C_WIDTH]
    k = xs[..., C_WIDTH:2 * C_WIDTH]
    v = xs[..., 2 * C_WIDTH:3 * C_WIDTH]
    wd = xs[..., 3 * C_WIDTH:3 * C_WIDTH + DECAY_LORA]
    ad = xs[..., 3 * C_WIDTH + DECAY_LORA:3 * C_WIDTH + DECAY_LORA + AAA_LORA]
    gd = xs[..., 3 * C_WIDTH + DECAY_LORA + AAA_LORA:]
    w = -jax.nn.softplus(-(w0.astype(f32) + jnp.tanh(wd) @ w_up.astype(f32))) - 0.5
    decay = jnp.exp(-jnp.exp(w))
    a = jax.nn.sigmoid(a0.astype(f32) + ad @ a_up.astype(f32))
    g = jax.nn.sigmoid(gd) @ g_up.astype(f32)
    heads = lambda z: z.reshape(B, T, H_C, HD_C)
    kk = heads(k * k_k.astype(f32))
    kk = kk / jnp.maximum(jnp.linalg.norm(kk, axis=-1, keepdims=True), 1e-12)
    k = k * (1.0 + (a - 1.0) * k_a.astype(f32))
    r_h, k_h, v_h, w_h, a_h = heads(r), heads(k), heads(v), heads(decay), heads(a)
    b_h = kk * a_h

    def step(S, inp):
        r_t, w_t, k_t, v_t, kk_t, b_t = inp
        sa = jnp.einsum('bhij,bhj->bhi', S, -kk_t)
        S = S * w_t[:, :, None, :] + sa[..., None] * b_t[:, :, None, :] + v_t[..., None] * k_t[:, :, None, :]
        return S, jnp.einsum('bhij,bhj->bhi', S, r_t)

    seq_in = (jnp.moveaxis(r_h, 1, 0), jnp.moveaxis(w_h, 1, 0), jnp.moveaxis(k_h, 1, 0),
              jnp.moveaxis(v_h, 1, 0), jnp.moveaxis(kk, 1, 0), jnp.moveaxis(b_h, 1, 0))
    S_fin, y = lax.scan(step, wkv_prev.astype(f32), seq_in)
    y = jnp.moveaxis(y, 0, 1)
    mean = jnp.mean(y, axis=-1, keepdims=True)
    var = jnp.mean(jnp.square(y - mean), axis=-1, keepdims=True)
    y = ((y - mean) * lax.rsqrt(var + LN_X_EPS)).reshape(B, T, C_WIDTH) * ln_w.astype(f32) + ln_b.astype(f32)
    bonus = jnp.sum(r_h * k_h * r_k.astype(f32), axis=-1, keepdims=True) * v_h
    y = (y + bonus.reshape(B, T, C_WIDTH)) * g
    return y.astype(pc.dtype), pc[:, -1], S_fin


def trunk_layer(x, prm, past_k, past_v, past_logf, pool_prev, shift_prev, wkv_prev):
    (n_mix_pre, n_mix_post, n_mlp_pre, n_mlp_post, w_in, b_forget, w_pool, pool_scale,
     mu, w0, w_up, a0, a_up, g_up, k_k, k_a, r_k, ln_w, ln_b,
     w_branch_a, w_branch_b, w_branch_c, w_out, w_mlp_up, w_mlp_down) = prm
    B, T, _ = x.shape
    P = past_k.shape[1]
    h = rms_norm(x, n_mix_pre)
    proj = h @ w_in
    u = proj[..., OFF_A:OFF_Q]
    q = proj[..., OFF_Q:OFF_K].reshape(B, T, H_B, HD_B)
    k = proj[..., OFF_K:OFF_V].reshape(B, T, H_B, HD_B)
    v = proj[..., OFF_V:OFF_F].reshape(B, T, H_B, HD_B)
    logf = jax.nn.log_sigmoid((proj[..., OFF_F:OFF_C] + b_forget).astype(jnp.float32))
    pc = proj[..., OFF_C:OFF_G]
    gates = jax.nn.sigmoid(proj[..., OFF_G:]).reshape(B, T, N_BRANCH, D_MODEL)
    ya, pool_new = pool_mix(u, pool_prev, P, w_pool, pool_scale)
    k_all = jnp.concatenate([past_k.astype(k.dtype), k], axis=1)
    v_all = jnp.concatenate([past_v.astype(v.dtype), v], axis=1)
    c_all = jnp.cumsum(jnp.concatenate([past_logf.astype(jnp.float32), logf], axis=1), axis=1)
    yb = fox_attention(q, k_all, v_all, c_all[:, P:], c_all, P).reshape(B, T, B_WIDTH)
    yc, shift_new, wkv_new = rwkv7_mix(pc, shift_prev, wkv_prev, mu, w0, w_up, a0, a_up, g_up,
                                       k_k, k_a, r_k, ln_w, ln_b)
    merged = (gates[:, :, 0] * (ya @ w_branch_a) + gates[:, :, 1] * (yb @ w_branch_b)
              + gates[:, :, 2] * (yc @ w_branch_c))
    x = x + rms_norm(merged @ w_out, n_mix_post)
    hm = rms_norm(x, n_mlp_pre)
    x = x + rms_norm(jnp.square(jax.nn.relu(hm @ w_mlp_up)) @ w_mlp_down, n_mlp_post)
    return x, (k, v, logf, pool_new, shift_new, wkv_new)


def gather_pages(cache_l, page_table):
    g = cache_l[page_table]
    return g.reshape((g.shape[0], g.shape[1] * g.shape[2]) + g.shape[3:])


def setup_inputs(seed: int = 0) -> dict:
    key = jax.random.key(seed)
    ks = iter(list(jax.random.split(key, 48)))
    f32 = jnp.float32
    nrm = lambda shape, scale: jax.random.normal(next(ks), shape, f32) * scale
    n_pages = PAST_LEN // PAGE_SIZE
    n_pool = (DEC_BATCH * n_pages * 5) // 4
    x_prompt = nrm((BATCH, SEQ, D_MODEL), 1.0)
    x_sample = nrm((DEC_BATCH, DEC_SEQ, D_MODEL), 1.0)
    cache_k = nrm((DEPTH, n_pool, PAGE_SIZE, H_B, HD_B), 1.0)
    cache_v = nrm((DEPTH, n_pool, PAGE_SIZE, H_B, HD_B), 1.0)
    cache_logf = jax.nn.log_sigmoid(3.0 + nrm((DEPTH, n_pool, PAGE_SIZE, H_B), 1.0))
    state_pool = nrm((DEPTH, DEC_BATCH, POOL_PREFIX, A_WIDTH), 1.0)
    state_shift = nrm((DEPTH, DEC_BATCH, C_IN), 1.0)
    state_wkv = nrm((DEPTH, DEC_BATCH, H_C, HD_C, HD_C), 0.5)
    page_table = jax.random.permutation(next(ks), n_pool)[:DEC_BATCH * n_pages].reshape(DEC_BATCH, n_pages).astype(jnp.int32)
    return {
        "x_prompt": x_prompt,
        "x_sample": x_sample,
        "cache_k": cache_k,
        "cache_v": cache_v,
        "cache_logf": cache_logf,
        "state_pool": state_pool,
        "state_shift": state_shift,
        "state_wkv": state_wkv,
        "page_table": page_table,
        "norm_mix_pre": 1.0 + nrm((DEPTH, D_MODEL), 0.05),
        "norm_mix_post": 1.0 + nrm((DEPTH, D_MODEL), 0.05),
        "norm_mlp_pre": 1.0 + nrm((DEPTH, D_MODEL), 0.05),
        "norm_mlp_post": 1.0 + nrm((DEPTH, D_MODEL), 0.05),
        "w_in": nrm((DEPTH, D_MODEL, IN_COLS), D_MODEL ** -0.5),
        "b_forget": 3.0 + nrm((DEPTH, H_B), 0.5),
        "w_pool": nrm((DEPTH, len(POOL_WINDOWS), POOL_GROUP_WIDTH, POOL_GROUP_WIDTH), POOL_GROUP_WIDTH ** -0.5),
        "pool_scale": 1.0 + nrm((DEPTH, A_WIDTH), 0.1),
        "rwkv_mu": jax.random.uniform(next(ks), (DEPTH, C_IN), f32),
        "rwkv_w0": -2.5 + nrm((DEPTH, C_WIDTH), 1.0),
        "rwkv_w_up": nrm((DEPTH, DECAY_LORA, C_WIDTH), 0.1),
        "rwkv_a0": nrm((DEPTH, C_WIDTH), 0.1),
        "rwkv_a_up": nrm((DEPTH, AAA_LORA, C_WIDTH), 0.5 * AAA_LORA ** -0.5),
        "rwkv_g_up": nrm((DEPTH, GATE_LORA, C_WIDTH), GATE_LORA ** -0.5),
        "rwkv_k_k": 0.85 + nrm((DEPTH, C_WIDTH), 0.05),
        "rwkv_k_a": 1.0 + nrm((DEPTH, C_WIDTH), 0.05),
        "rwkv_r_k": nrm((DEPTH, H_C, HD_C), 0.1),
        "rwkv_ln_w": 1.0 + nrm((DEPTH, C_WIDTH), 0.05),
        "rwkv_ln_b": nrm((DEPTH, C_WIDTH), 0.02),
        "w_branch_a": nrm((DEPTH, A_WIDTH, D_MODEL), A_WIDTH ** -0.5),
        "w_branch_b": nrm((DEPTH, B_WIDTH, D_MODEL), B_WIDTH ** -0.5),
        "w_branch_c": nrm((DEPTH, C_WIDTH, D_MODEL), C_WIDTH ** -0.5),
        "w_out": nrm((DEPTH, D_MODEL, D_MODEL), D_MODEL ** -0.5),
        "w_mlp_up": nrm((DEPTH, D_MODEL, D_FF), D_MODEL ** -0.5),
        "w_mlp_down": nrm((DEPTH, D_FF, D_MODEL), D_FF ** -0.5),
    }


def reference(x_prompt, x_sample, cache_k, cache_v, cache_logf, state_pool, state_shift, state_wkv,
              page_table, norm_mix_pre, norm_mix_post, norm_mlp_pre, norm_mlp_post, w_in, b_forget,
              w_pool, pool_scale, rwkv_mu, rwkv_w0, rwkv_w_up, rwkv_a0, rwkv_a_up, rwkv_g_up,
              rwkv_k_k, rwkv_k_a, rwkv_r_k, rwkv_ln_w, rwkv_ln_b, w_branch_a, w_branch_b,
              w_branch_c, w_out, w_mlp_up, w_mlp_down):
    yp, ys = x_prompt, x_sample
    bp = x_prompt.shape[0]
    pk, pv, pf, ppool, pshift, pwkv = [], [], [], [], [], []
    sk, sv, sf, spool, sshift, swkv = [], [], [], [], [], []
    for l in range(DEPTH):
        prm = (norm_mix_pre[l], norm_mix_post[l], norm_mlp_pre[l], norm_mlp_post[l], w_in[l], b_forget[l],
               w_pool[l], pool_scale[l], rwkv_mu[l], rwkv_w0[l], rwkv_w_up[l], rwkv_a0[l], rwkv_a_up[l],
               rwkv_g_up[l], rwkv_k_k[l], rwkv_k_a[l], rwkv_r_k[l], rwkv_ln_w[l], rwkv_ln_b[l],
               w_branch_a[l], w_branch_b[l], w_branch_c[l], w_out[l], w_mlp_up[l], w_mlp_down[l])
        yp, (k_, v_, f_, po_, sh_, wk_) = trunk_layer(
            yp, prm,
            jnp.zeros((bp, 0, H_B, HD_B), yp.dtype), jnp.zeros((bp, 0, H_B, HD_B), yp.dtype),
            jnp.zeros((bp, 0, H_B), jnp.float32), jnp.zeros((bp, POOL_PREFIX, A_WIDTH), yp.dtype),
            jnp.zeros((bp, C_IN), yp.dtype), jnp.zeros((bp, H_C, HD_C, HD_C), jnp.float32))
        pk.append(k_); pv.append(v_); pf.append(f_); ppool.append(po_); pshift.append(sh_); pwkv.append(wk_)
        ys, (k_, v_, f_, po_, sh_, wk_) = trunk_layer(
            ys, prm,
            gather_pages(cache_k[l], page_table), gather_pages(cache_v[l], page_table),
            gather_pages(cache_logf[l], page_table), state_pool[l], state_shift[l], state_wkv[l])
        sk.append(k_); sv.append(v_); sf.append(f_); spool.append(po_); sshift.append(sh_); swkv.append(wk_)
    return (yp, ys,
            jnp.stack(pk), jnp.stack(pv), jnp.stack(pf), jnp.stack(ppool), jnp.stack(pshift), jnp.stack(pwkv),
            jnp.stack(sk), jnp.stack(sv), jnp.stack(sf), jnp.stack(spool), jnp.stack(sshift), jnp.stack(swkv))
```

```python
import functools

import jax
import jax.numpy as jnp
from jax import lax
from jax.experimental import pallas as pl
from jax.experimental.pallas import tpu as pltpu

F32 = jnp.float32
BF16 = jnp.bfloat16

D_MODEL = 1024
DEPTH = 4
PAGE_SIZE = 128
POOL_WINDOWS = (2, 4, 8, 16)
A_WIDTH = 512
GROUP_W = 128
POOL_PREFIX = 15
HD = 64
NH = 8
WIDTH = 512
NHP = NH // 2
LANES = 128
NEG_INF = -1e30
DECAY_LORA = 64
AAA_LORA = 64
GATE_LORA = 128
C_IN = 3 * WIDTH + DECAY_LORA + AAA_LORA + GATE_LORA
LN_X_EPS = 64e-5
D_FF = 4 * D_MODEL
RMS_EPS = 1e-6
OFF_F = 4 * WIDTH
OFF_C = OFF_F + NH
OFF_G = OFF_C + C_IN
ATT_SCALE = HD ** -0.5
VMEM_LIMIT = 56 * 1024 * 1024


def _dot(a, b):
    return jnp.dot(a, b, preferred_element_type=F32)


def _dot_nt(a, b):
    return lax.dot_general(a, b, (((1,), (1,)), ((), ())), preferred_element_type=F32)


def _split2(x):
    hi = x.astype(BF16)
    lo = (x - hi.astype(F32)).astype(BF16)
    return hi, lo


def _dot_left2(x, m):
    hi, lo = _split2(x)
    return _dot(hi, m) + _dot(lo, m)


def _dot_left3(x, m):
    hi = x.astype(BF16)
    r1 = x - hi.astype(F32)
    mid = r1.astype(BF16)
    lo = (r1 - mid.astype(F32)).astype(BF16)
    return _dot(hi, m) + _dot(mid, m) + _dot(lo, m)


def _dot_right3(m, x):
    hi = x.astype(BF16)
    r1 = x - hi.astype(F32)
    mid = r1.astype(BF16)
    lo = (r1 - mid.astype(F32)).astype(BF16)
    return _dot(m, hi) + _dot(m, mid) + _dot(m, lo)


def _rms(x, g):
    ms = jnp.mean(x * x, axis=-1, keepdims=True)
    return x * lax.rsqrt(ms + RMS_EPS) * g


def _sigmoid(x):
    return 1.0 / (1.0 + jnp.exp(-x))


def _softplus(x):
    return jnp.maximum(x, 0.0) + jnp.log1p(jnp.exp(-jnp.abs(x)))


def _params(sem):
    return pltpu.CompilerParams(dimension_semantics=sem, vmem_limit_bytes=VMEM_LIMIT)


def _const_spec(shape):
    nd = len(shape)
    return pl.BlockSpec(shape, lambda *_: (0,) * nd, pipeline_mode=pl.Buffered(1))


def _row_tile(n, pref):
    return pref if n % pref == 0 else n


def _in_proj_kernel(x_ref, g_ref, wm_ref, wf_ref, bf_ref,
                    u_ref, q_ref, k_ref, v_ref, kb_ref, vb_ref, pc_ref, lf_ref):
    h = _rms(x_ref[...], g_ref[...]).astype(BF16)
    u_ref[...] = _dot(h, wm_ref[:, 0:WIDTH])
    q_ref[...] = (_dot(h, wm_ref[:, WIDTH:2 * WIDTH]) * ATT_SCALE).astype(BF16)
    k = _dot(h, wm_ref[:, 2 * WIDTH:3 * WIDTH])
    k_ref[...] = k
    kb_ref[...] = k.astype(BF16)
    v = _dot(h, wm_ref[:, 3 * WIDTH:4 * WIDTH])
    v_ref[...] = v
    vb_ref[...] = v.astype(BF16)
    pc_ref[...] = _dot(h, wm_ref[:, 4 * WIDTH:4 * WIDTH + C_IN])
    f = _dot(h, wf_ref[...]) + bf_ref[...]
    lf_ref[...] = -_softplus(-f)


def _in_proj(x, g, wm, wf, bf):
    n = x.shape[0]
    tm = _row_tile(n, 512)
    row = lambda w: pl.BlockSpec((tm, w), lambda i: (i, 0))
    out_shape = (
        jax.ShapeDtypeStruct((n, WIDTH), F32),
        jax.ShapeDtypeStruct((n, WIDTH), BF16),
        jax.ShapeDtypeStruct((n, WIDTH), F32),
        jax.ShapeDtypeStruct((n, WIDTH), F32),
        jax.ShapeDtypeStruct((n, WIDTH), BF16),
        jax.ShapeDtypeStruct((n, WIDTH), BF16),
        jax.ShapeDtypeStruct((n, C_IN), F32),
        jax.ShapeDtypeStruct((n, LANES), F32),
    )
    return pl.pallas_call(
        _in_proj_kernel,
        grid=(n // tm,),
        in_specs=[row(D_MODEL), _const_spec((1, D_MODEL)), _const_spec(wm.shape),
                  _const_spec(wf.shape), _const_spec((1, LANES))],
        out_specs=(row(WIDTH), row(WIDTH), row(WIDTH), row(WIDTH), row(WIDTH), row(WIDTH),
                   row(C_IN), row(LANES)),
        out_shape=out_shape,
        compiler_params=_params(("parallel",)),
        name="in_proj",
    )(x, g, wm, wf, bf)


CUM_BLK = 256


def _cumsum_kernel(lf_ref, tri_ref, c_ref, ct_ref):
    t = lf_ref.shape[1]
    carry = jnp.zeros((1, LANES), F32)
    for i in range(t // CUM_BLK):
        sl = slice(i * CUM_BLK, (i + 1) * CUM_BLK)
        c = _dot_right3(tri_ref[...], lf_ref[0, sl, :]) + carry
        c_ref[0, sl, :] = c
        ct_ref[0, :, sl] = jnp.transpose(c)[0:NH, :]
        carry = c[CUM_BLK - 1:CUM_BLK, :]


def _cumsum(lf, tri):
    b, t, _ = lf.shape
    return pl.pallas_call(
        _cumsum_kernel,
        grid=(b,),
        in_specs=[pl.BlockSpec((1, t, LANES), lambda i: (i, 0, 0)), _const_spec(tri.shape)],
        out_specs=(pl.BlockSpec((1, t, LANES), lambda i: (i, 0, 0)),
                   pl.BlockSpec((1, NH, t), lambda i: (i, 0, 0))),
        out_shape=(jax.ShapeDtypeStruct((b, t, LANES), F32),
                   jax.ShapeDtypeStruct((b, NH, t), F32)),
        compiler_params=_params(("parallel",)),
        name="fox_cumsum",
    )(lf, tri)


def _flash_kernel(q_ref, k_ref, v_ref, cq_ref, ck_ref, o_ref, m_scr, l_scr, acc_scr, *, tq, tk):
    qi = pl.program_id(1)
    ki = pl.program_id(2)

    @pl.when(ki == 0)
    def _():
        m_scr[...] = jnp.full(m_scr.shape, NEG_INF, F32)
        l_scr[...] = jnp.zeros(l_scr.shape, F32)
        acc_scr[...] = jnp.zeros(acc_scr.shape, F32)

    @pl.when(ki <= qi)
    def _():
        row = qi * tq + lax.broadcasted_iota(jnp.int32, (tq, tk), 0)
        col = ki * tk + lax.broadcasted_iota(jnp.int32, (tq, tk), 1)
        causal = col <= row
        lane = lax.broadcasted_iota(jnp.int32, (1, LANES), 1)
        cq = cq_ref[0]
        ck = ck_ref[0]
        for hp in range(NHP):
            ls = slice(hp * LANES, (hp + 1) * LANES)
            q = q_ref[0, :, ls]
            k = k_ref[0, :, ls]
            v = v_ref[0, :, ls]
            pv = []
            alpha = []
            for j in range(2):
                h = 2 * hp + j
                qm = q * ((lane // HD) == j).astype(F32).astype(BF16)
                s = _dot_nt(qm, k)
                s = s + (cq[:, h:h + 1] - ck[h:h + 1, :])
                s = jnp.where(causal, s, NEG_INF)
                m_prev = m_scr[h]
                m_new = jnp.maximum(m_prev, jnp.max(s, axis=1, keepdims=True))
                a = jnp.exp(m_prev - m_new)
                p = jnp.exp(s - m_new[:, 0:1])
                l_scr[h] = a * l_scr[h] + jnp.sum(p, axis=1, keepdims=True)
                m_scr[h] = m_new
                pv.append(_dot(p.astype(BF16), v))
                alpha.append(a)
            first = (lane // HD) == 0
            acc_scr[hp] = (jnp.where(first, alpha[0], alpha[1]) * acc_scr[hp]
                           + jnp.where(first, pv[0], pv[1]))

    @pl.when(ki == qi)
    def _():
        lane = lax.broadcasted_iota(jnp.int32, (1, LANES), 1)
        for hp in range(NHP):
            l = jnp.where((lane // HD) == 0, l_scr[2 * hp], l_scr[2 * hp + 1])
            o_ref[0, :, hp * LANES:(hp + 1) * LANES] = (acc_scr[hp] / l).astype(o_ref.dtype)


def _flash(q, k, v, cq, ck, *, tq):
    b, t, _ = q.shape
    tq = min(tq, t)
    nq = t // tq
    kv_idx = lambda bi, qi, ki: (bi, jnp.minimum(ki, qi), 0)
    return pl.pallas_call(
        functools.partial(_flash_kernel, tq=tq, tk=tq),
        grid=(b, nq, nq),
        in_specs=[pl.BlockSpec((1, tq, WIDTH), lambda bi, qi, ki: (bi, qi, 0)),
                  pl.BlockSpec((1, tq, WIDTH), kv_idx),
                  pl.BlockSpec((1, tq, WIDTH), kv_idx),
                  pl.BlockSpec((1, tq, LANES), lambda bi, qi, ki: (bi, qi, 0)),
                  pl.BlockSpec((1, NH, tq), lambda bi, qi, ki: (bi, 0, jnp.minimum(ki, qi)))],
        out_specs=pl.BlockSpec((1, tq, WIDTH), lambda bi, qi, ki: (bi, qi, 0)),
        out_shape=jax.ShapeDtypeStruct((b, t, WIDTH), BF16),
        scratch_shapes=[pltpu.VMEM((NH, tq, LANES), F32), pltpu.VMEM((NH, tq, LANES), F32),
                        pltpu.VMEM((NHP, tq, LANES), F32)],
        compiler_params=_params(("parallel", "parallel", "arbitrary")),
        name="fox_flash",
    )(q, k, v, cq, ck)


def _decode_bias_kernel(pt_ref, tbl_ref, lfn_ref, msuf_ref, mtot_ref, plater_ref, o_ref, g_scr):
    b = pl.program_id(0)
    npages = g_scr.shape[0]
    for p in range(npages):
        g_scr[p:p + 1, :] = tbl_ref[pl.ds(pt_ref[b, p], 1), :]
    g = g_scr[...]
    within = _dot_left3(g, msuf_ref[...])
    tot = _dot_left3(g, mtot_ref[...])
    later = _dot_right3(plater_ref[...], tot)
    o_ref[0] = within + later + lfn_ref[0]


def _decode_bias(page_table, tbl, lfn_t, msuf, mtot, plater):
    nb, npages = page_table.shape
    w = PAGE_SIZE * NH
    grid_spec = pltpu.PrefetchScalarGridSpec(
        num_scalar_prefetch=1,
        grid=(nb,),
        in_specs=[_const_spec(tbl.shape),
                  pl.BlockSpec((1, 1, w), lambda b, pt: (b, 0, 0)),
                  _const_spec(msuf.shape), _const_spec(mtot.shape), _const_spec(plater.shape)],
        out_specs=pl.BlockSpec((1, npages, w), lambda b, pt: (b, 0, 0)),
        scratch_shapes=[pltpu.VMEM((npages, w), F32)],
    )
    return pl.pallas_call(
        _decode_bias_kernel,
        grid_spec=grid_spec,
        out_shape=jax.ShapeDtypeStruct((nb, npages, w), F32),
        compiler_params=_params(("arbitrary",)),
        name="fox_decode_bias",
    )(page_table, tbl, lfn_t, msuf, mtot, plater)


PAGES_PER_STEP = 8


def _decode_attn_kernel(pt_ref, q_ref, kn_ref, vn_ref, bias_ref, *refs):
    k_refs = refs[0:PAGES_PER_STEP]
    v_refs = refs[PAGES_PER_STEP:2 * PAGES_PER_STEP]
    o_ref, m_scr, l_scr, acc_scr = refs[2 * PAGES_PER_STEP:]
    g = pl.program_id(1)
    head_of_lane = lax.broadcasted_iota(jnp.int32, (NH, WIDTH), 1) // HD
    head_of_row = lax.broadcasted_iota(jnp.int32, (NH, WIDTH), 0)
    own = head_of_lane == head_of_row
    q = q_ref[0].astype(F32)
    qbd_f = jnp.where(own, jnp.broadcast_to(q, (NH, WIDTH)), 0.0)
    qbd = qbd_f.astype(BF16)

    @pl.when(g == 0)
    def _():
        s_new = jnp.sum(qbd_f * kn_ref[0], axis=1, keepdims=True)
        m_scr[...] = jnp.broadcast_to(s_new, m_scr.shape)
        l_scr[...] = jnp.ones(l_scr.shape, F32)
        acc_scr[...] = jnp.broadcast_to(vn_ref[0], acc_scr.shape)

    s_parts = []
    for j in range(PAGES_PER_STEP):
        kj = k_refs[j][0, 0].astype(BF16)
        s_parts.append(_dot_nt(qbd, kj) + bias_ref[0, j])
    s = jnp.concatenate(s_parts, axis=1)
    m_prev = m_scr[...]
    m_new = jnp.maximum(m_prev, jnp.max(s, axis=1, keepdims=True))
    a = jnp.exp(m_prev - m_new)
    p = jnp.exp(s - m_new[:, 0:1])
    l_scr[...] = a * l_scr[...] + jnp.sum(p, axis=1, keepdims=True)
    m_scr[...] = m_new
    pb = p.astype(BF16)
    pv = jnp.zeros((NH, WIDTH), F32)
    for j in range(PAGES_PER_STEP):
        vj = v_refs[j][0, 0].astype(BF16)
        pv = pv + _dot(pb[:, j * PAGE_SIZE:(j + 1) * PAGE_SIZE], vj)
    acc_scr[...] = acc_scr[...] * jnp.broadcast_to(a[:, 0:1], (NH, WIDTH)) + pv

    @pl.when(g == pl.num_programs(1) - 1)
    def _():
        inv_l = jnp.broadcast_to(1.0 / l_scr[:, 0:1], (NH, WIDTH))
        o = jnp.where(own, acc_scr[...] * inv_l, 0.0)
        o_ref[0] = jnp.sum(o, axis=0, keepdims=True)


def _decode_attn(page_table, q, k_new, v_new, bias, cache_k, cache_v, layer):
    nb, npages = page_table.shape
    ng = npages // PAGES_PER_STEP

    def page_spec(j):
        return pl.BlockSpec((1, 1, PAGE_SIZE, WIDTH),
                            lambda b, g, pt: (layer, pt[b, g * PAGES_PER_STEP + j], 0, 0))

    vec = lambda: pl.BlockSpec((1, 1, WIDTH), lambda b, g, pt: (b, 0, 0))
    grid_spec = pltpu.PrefetchScalarGridSpec(
        num_scalar_prefetch=1,
        grid=(nb, ng),
        in_specs=[vec(), vec(), vec(),
                  pl.BlockSpec((1, PAGES_PER_STEP, NH, PAGE_SIZE), lambda b, g, pt: (b, g, 0, 0))]
                 + [page_spec(j) for j in range(PAGES_PER_STEP)]
                 + [page_spec(j) for j in range(PAGES_PER_STEP)],
        out_specs=pl.BlockSpec((1, 1, WIDTH), lambda b, g, pt: (b, 0, 0)),
        scratch_shapes=[pltpu.VMEM((NH, LANES), F32), pltpu.VMEM((NH, LANES), F32),
                        pltpu.VMEM((NH, WIDTH), F32)],
    )
    return pl.pallas_call(
        _decode_attn_kernel,
        grid_spec=grid_spec,
        out_shape=jax.ShapeDtypeStruct((nb, 1, WIDTH), F32),
        compiler_params=_params(("parallel", "arbitrary")),
        name="fox_decode_attn",
    )(page_table, q, k_new, v_new, bias,
      *([cache_k] * PAGES_PER_STEP), *([cache_v] * PAGES_PER_STEP))


POOL_HALO = 16


def _pool_project(pooled, wp_ref, sc_ref):
    outs = []
    for g in range(len(POOL_WINDOWS)):
        ls = slice(g * GROUP_W, (g + 1) * GROUP_W)
        outs.append(_dot(pooled[g].astype(BF16), wp_ref[g]) * sc_ref[:, ls])
    return outs


def _pool_prompt_kernel(u_ref, wp_ref, sc_ref, o_ref, ext_scr, *, tt):
    ti = pl.program_id(1)

    @pl.when(ti == 0)
    def _():
        ext_scr[0:POOL_HALO, :] = jnp.zeros((POOL_HALO, A_WIDTH), F32)

    ext_scr[POOL_HALO:POOL_HALO + tt, :] = u_ref[0]
    pos1 = ti * tt + lax.broadcasted_iota(jnp.int32, (tt, GROUP_W), 0) + 1
    pooled = []
    for g, w in enumerate(POOL_WINDOWS):
        ls = slice(g * GROUP_W, (g + 1) * GROUP_W)
        x = ext_scr[POOL_HALO:POOL_HALO + tt, ls]
        win = x
        for d in range(1, w):
            win = win + ext_scr[POOL_HALO - d:POOL_HALO - d + tt, ls]
        cnt = jnp.minimum(w, pos1).astype(F32)
        pooled.append(win / cnt - x)
    outs = _pool_project(pooled, wp_ref, sc_ref)
    for g in range(len(POOL_WINDOWS)):
        o_ref[0, :, g * GROUP_W:(g + 1) * GROUP_W] = outs[g].astype(o_ref.dtype)
    ext_scr[0:POOL_HALO, :] = ext_scr[tt:tt + POOL_HALO, :]


def _pool_prompt(u, wp, sc):
    b, t, _ = u.shape
    tt = _row_tile(t, 512)
    return pl.pallas_call(
        functools.partial(_pool_prompt_kernel, tt=tt),
        grid=(b, t // tt),
        in_specs=[pl.BlockSpec((1, tt, A_WIDTH), lambda bi, ti: (bi, ti, 0)),
                  _const_spec(wp.shape), _const_spec((1, A_WIDTH))],
        out_specs=pl.BlockSpec((1, tt, A_WIDTH), lambda bi, ti: (bi, ti, 0)),
        out_shape=jax.ShapeDtypeStruct((b, t, A_WIDTH), BF16),
        scratch_shapes=[pltpu.VMEM((POOL_HALO + tt, A_WIDTH), F32)],
        compiler_params=_params(("parallel", "arbitrary")),
        name="pool_prompt",
    )(u, wp, sc)


def _pool_sample_kernel(u_ref, prev_ref, wp_ref, sc_ref, o_ref):
    pooled = []
    for g, w in enumerate(POOL_WINDOWS):
        ls = slice(g * GROUP_W, (g + 1) * GROUP_W)
        x = u_ref[:, ls]
        win = x
        for d in range(1, w):
            win = win + prev_ref[POOL_PREFIX - d, :, ls]
        pooled.append(win / float(w) - x)
    outs = _pool_project(pooled, wp_ref, sc_ref)
    for g in range(len(POOL_WINDOWS)):
        o_ref[:, g * GROUP_W:(g + 1) * GROUP_W] = outs[g]


def _pool_sample(u, prev_t, wp, sc):
    nb = u.shape[0]
    return pl.pallas_call(
        _pool_sample_kernel,
        out_shape=jax.ShapeDtypeStruct((nb, A_WIDTH), F32),
        compiler_params=pltpu.CompilerParams(vmem_limit_bytes=VMEM_LIMIT),
        name="pool_sample",
    )(u, prev_t, wp, sc)


SCAN_BB = 8


def _rwkv_prep_math(pc, prev, mu_ref, wlo_ref, w0_ref, a0_ref, gup_ref, kk_ref, ka_ref, rk_ref,
                    seg_ref):
    xs = pc + (prev - pc) * mu_ref[...]
    r = xs[:, 0:WIDTH]
    k = xs[:, WIDTH:2 * WIDTH]
    v = xs[:, 2 * WIDTH:3 * WIDTH]
    lora = xs[:, 3 * WIDTH:3 * WIDTH + LANES]
    gd = xs[:, 3 * WIDTH + LANES:3 * WIDTH + 2 * LANES]
    lane = lax.broadcasted_iota(jnp.int32, (1, LANES), 1)
    lhs = jnp.where(lane < DECAY_LORA, jnp.tanh(lora), lora).astype(BF16)
    up = _dot(lhs, wlo_ref[...])
    w = -_softplus(-(w0_ref[...] + up[:, 0:WIDTH])) - 0.5
    decay = jnp.exp(-jnp.exp(w))
    a = _sigmoid(a0_ref[...] + up[:, WIDTH:2 * WIDTH])
    g = _dot(_sigmoid(gd).astype(BF16), gup_ref[...])
    kk = k * kk_ref[...]
    nrm = jnp.sqrt(_dot_left2(kk * kk, seg_ref[...]))
    kk = kk / jnp.maximum(nrm, 1e-12)
    k2 = k * (1.0 + (a - 1.0) * ka_ref[...])
    bonus = _dot_left2(r * k2 * rk_ref[...], seg_ref[...]) * v
    return r, decay, k2, v, -kk, kk * a, g, bonus


def _rwkv_prep_prompt_kernel(pc_ref, mu_ref, wlo_ref, w0_ref, a0_ref, gup_ref, kk_ref, ka_ref,
                             rk_ref, seg_ref, *refs, tt):
    out_refs, last_scr = refs[:-1], refs[-1]
    ti = pl.program_id(0)

    @pl.when(ti == 0)
    def _():
        last_scr[...] = jnp.zeros(last_scr.shape, F32)

    pc3 = pc_ref[...]
    prev3 = jnp.concatenate([last_scr[...], pc3[0:tt - 1]], axis=0)
    last_scr[...] = pc3[tt - 1:tt]
    rows = tt * SCAN_BB
    outs = _rwkv_prep_math(pc3.reshape(rows, C_IN), prev3.reshape(rows, C_IN), mu_ref, wlo_ref,
                           w0_ref, a0_ref, gup_ref, kk_ref, ka_ref, rk_ref, seg_ref)
    for o_ref, val in zip(out_refs, outs):
        o_ref[...] = val.reshape(tt, SCAN_BB, WIDTH)


def _rwkv_prep_prompt(pc_tm, weights):
    t, b, _ = pc_tm.shape
    assert b == SCAN_BB
    tt = _row_tile(t, 64)
    tile = lambda w: pl.BlockSpec((tt, b, w), lambda ti: (ti, 0, 0))
    return pl.pallas_call(
        functools.partial(_rwkv_prep_prompt_kernel, tt=tt),
        grid=(t // tt,),
        in_specs=[tile(C_IN)] + [_const_spec(w.shape) for w in weights],
        out_specs=tuple(tile(WIDTH) for _ in range(8)),
        out_shape=tuple(jax.ShapeDtypeStruct((t, b, WIDTH), F32) for _ in range(8)),
        scratch_shapes=[pltpu.VMEM((1, b, C_IN), F32)],
        compiler_params=_params(("arbitrary",)),
        name="rwkv_prep_prompt",
    )(pc_tm, *weights)


def _rwkv_prep_sample_kernel(pc_ref, prev_ref, mu_ref, wlo_ref, w0_ref, a0_ref, gup_ref, kk_ref,
                             ka_ref, rk_ref, seg_ref, *out_refs):
    outs = _rwkv_prep_math(pc_ref[...], prev_ref[...], mu_ref, wlo_ref, w0_ref, a0_ref, gup_ref,
                           kk_ref, ka_ref, rk_ref, seg_ref)
    for o_ref, val in zip(out_refs, outs):
        o_ref[...] = val


def _rwkv_prep_sample(pc, prev, weights):
    nb = pc.shape[0]
    return pl.pallas_call(
        _rwkv_prep_sample_kernel,
        out_shape=tuple(jax.ShapeDtypeStruct((nb, WIDTH), F32) for _ in range(8)),
        compiler_params=pltpu.CompilerParams(vmem_limit_bytes=VMEM_LIMIT),
        name="rwkv_prep_sample",
    )(pc, prev, *weights)


def _scan_kernel(r_ref, w_ref, k_ref, v_ref, kn_ref, b_ref, s0_ref, seg_ref, eye_ref,
                 y_ref, so_ref, s_scr, *, tc, zero_init):
    ci = pl.program_id(1)

    @pl.when(ci == 0)
    def _():
        if zero_init:
            s_scr[...] = jnp.zeros(s_scr.shape, F32)
        else:
            s_scr[...] = s0_ref[...]

    seg = seg_ref[...]
    eye2 = eye_ref[...]
    groups = [(b, hp) for b in range(SCAN_BB) for hp in range(NHP)]
    sub = lax.broadcasted_iota(jnp.int32, (SCAN_BB, LANES), 0)

    def row(slab, b, hp):
        return jnp.broadcast_to(slab[b:b + 1, hp * LANES:(hp + 1) * LANES], (HD, LANES))

    def step(t, carry):
        r_t, w_t, k_t, v_t, kn_t, b_t = (ref[t] for ref in
                                         (r_ref, w_ref, k_ref, v_ref, kn_ref, b_ref))
        xs, vd = [], []
        for b, hp in groups:
            xs.append(s_scr[b, hp] * row(kn_t, b, hp))
            vd.append(eye2 * row(v_t, b, hp))
        sa = _dot_left2(jnp.concatenate(xs, axis=0), seg)
        vb = _dot_left2(jnp.concatenate(vd, axis=0), seg)
        x2 = []
        for n, (b, hp) in enumerate(groups):
            rs = slice(n * HD, (n + 1) * HD)
            s_new = (s_scr[b, hp] * row(w_t, b, hp) + sa[rs] * row(b_t, b, hp)
                     + vb[rs] * row(k_t, b, hp))
            s_scr[b, hp] = s_new
            x2.append(s_new * row(r_t, b, hp))
        yb = _dot_left2(jnp.concatenate(x2, axis=0), seg)
        y_tiles = [jnp.zeros((SCAN_BB, LANES), F32) for _ in range(NHP)]
        for n, (b, hp) in enumerate(groups):
            rs = slice(n * HD, (n + 1) * HD)
            y_row = jnp.sum(yb[rs] * eye2, axis=0, keepdims=True)
            y_tiles[hp] = jnp.where(sub == b, jnp.broadcast_to(y_row, (SCAN_BB, LANES)),
                                    y_tiles[hp])
        y_ref[t] = jnp.concatenate(y_tiles, axis=1)
        return carry

    lax.fori_loop(0, tc, step, 0)

    @pl.when(ci == pl.num_programs(1) - 1)
    def _():
        so_ref[...] = s_scr[...]


def _scan(r, w, k, v, kn, bv, s0, seg2, eye2, *, tc):
    t, nb, _ = r.shape
    tc = min(tc, t)
    zero_init = s0 is None
    if zero_init:
        s0 = jnp.zeros((SCAN_BB, NHP, HD, LANES), F32)
        s0_spec = pl.BlockSpec((SCAN_BB, NHP, HD, LANES), lambda bi, ci: (0, 0, 0, 0))
    else:
        s0_spec = pl.BlockSpec((SCAN_BB, NHP, HD, LANES), lambda bi, ci: (bi, 0, 0, 0))
    seq = lambda: pl.BlockSpec((tc, SCAN_BB, WIDTH), lambda bi, ci: (ci, bi, 0))
    return pl.pallas_call(
        functools.partial(_scan_kernel, tc=tc, zero_init=zero_init),
        grid=(nb // SCAN_BB, t // tc),
        in_specs=[seq(), seq(), seq(), seq(), seq(), seq(), s0_spec,
                  _const_spec(seg2.shape), _const_spec(eye2.shape)],
        out_specs=(seq(), pl.BlockSpec((SCAN_BB, NHP, HD, LANES), lambda bi, ci: (bi, 0, 0, 0))),
        out_shape=(jax.ShapeDtypeStruct((t, nb, WIDTH), F32),
                   jax.ShapeDtypeStruct((nb, NHP, HD, LANES), F32)),
        scratch_shapes=[pltpu.VMEM((SCAN_BB, NHP, HD, LANES), F32)],
        compiler_params=_params(("parallel", "arbitrary")),
        name="rwkv_scan",
    )(r, w, k, v, kn, bv, s0, seg2, eye2)


def _rwkv_post_kernel(y_ref, bonus_ref, g_ref, lnw_ref, lnb_ref, seg_ref, o_ref):
    y = y_ref[...]
    seg = seg_ref[...]
    mean = _dot_left2(y, seg) * (1.0 / HD)
    d = y - mean
    var = _dot_left2(d * d, seg) * (1.0 / HD)
    yn = d * lax.rsqrt(var + LN_X_EPS) * lnw_ref[...] + lnb_ref[...]
    o_ref[...] = ((yn + bonus_ref[...]) * g_ref[...]).astype(o_ref.dtype)


def _rwkv_post(y, bonus, g, lnw, lnb, seg, out_dtype):
    n = y.shape[0]
    tm = _row_tile(n, 512)
    row = pl.BlockSpec((tm, WIDTH), lambda i: (i, 0))
    return pl.pallas_call(
        _rwkv_post_kernel,
        grid=(n // tm,),
        in_specs=[row, row, row, _const_spec((1, WIDTH)), _const_spec((1, WIDTH)),
                  _const_spec(seg.shape)],
        out_specs=row,
        out_shape=jax.ShapeDtypeStruct((n, WIDTH), out_dtype),
        compiler_params=_params(("parallel",)),
        name="rwkv_post",
    )(y, bonus, g, lnw, lnb, seg)


def _merge_kernel(x_ref, ya_ref, yb_ref, yc_ref, gpre_ref, gpost_ref, wg_ref, wa_ref, wb_ref,
                  wc_ref, wo_ref, o_ref):
    x = x_ref[...]
    h = _rms(x, gpre_ref[...]).astype(BF16)
    merged = None
    for n, (y_ref, w_ref) in enumerate(((ya_ref, wa_ref), (yb_ref, wb_ref), (yc_ref, wc_ref))):
        gate = _sigmoid(_dot(h, wg_ref[:, n * D_MODEL:(n + 1) * D_MODEL]))
        term = gate * _dot(y_ref[...].astype(BF16), w_ref[...])
        merged = term if merged is None else merged + term
    o = _dot(merged.astype(BF16), wo_ref[...])
    o_ref[...] = x + _rms(o, gpost_ref[...])


def _merge(x, ya, yb, yc, gpre, gpost, wg, wa, wb, wc, wo):
    n = x.shape[0]
    tm = _row_tile(n, 512)
    row = lambda w: pl.BlockSpec((tm, w), lambda i: (i, 0))
    return pl.pallas_call(
        _merge_kernel,
        grid=(n // tm,),
        in_specs=[row(D_MODEL), row(WIDTH), row(WIDTH), row(WIDTH),
                  _const_spec((1, D_MODEL)), _const_spec((1, D_MODEL)),
                  _const_spec(wg.shape), _const_spec(wa.shape), _const_spec(wb.shape),
                  _const_spec(wc.shape), _const_spec(wo.shape)],
        out_specs=row(D_MODEL),
        out_shape=jax.ShapeDtypeStruct((n, D_MODEL), F32),
        compiler_params=_params(("parallel",)),
        name="merge",
    )(x, ya, yb, yc, gpre, gpost, wg, wa, wb, wc, wo)


def _mlp_kernel(x_ref, gpre_ref, gpost_ref, wu_ref, wd_ref, o_ref):
    x = x_ref[...]
    h = _rms(x, gpre_ref[...]).astype(BF16)
    up = jnp.maximum(_dot(h, wu_ref[...]), 0.0)
    down = _dot((up * up).astype(BF16), wd_ref[...])
    o_ref[...] = x + _rms(down, gpost_ref[...])


def _mlp(x, gpre, gpost, wu, wd):
    n = x.shape[0]
    tm = _row_tile(n, 512)
    row = pl.BlockSpec((tm, D_MODEL), lambda i: (i, 0))
    return pl.pallas_call(
        _mlp_kernel,
        grid=(n // tm,),
        in_specs=[row, _const_spec((1, D_MODEL)), _const_spec((1, D_MODEL)),
                  _const_spec(wu.shape), _const_spec(wd.shape)],
        out_specs=row,
        out_shape=jax.ShapeDtypeStruct((n, D_MODEL), F32),
        compiler_params=_params(("parallel",)),
        name="mlp",
    )(x, gpre, gpost, wu, wd)


def _seg_ones(n):
    i = jnp.arange(n)
    return (i[:, None] // HD == i[None, :] // HD).astype(BF16)


def _decode_bias_mats(npages):
    w = PAGE_SIZE * NH
    src = jnp.arange(w)
    dst = jnp.arange(w)
    same_head = (src % NH)[:, None] == (dst // PAGE_SIZE)[None, :]
    after = (src // NH)[:, None] > (dst % PAGE_SIZE)[None, :]
    msuf = (same_head & after).astype(BF16)
    mtot = same_head.astype(BF16)
    p = jnp.arange(npages)
    plater = (p[None, :] > p[:, None]).astype(BF16)
    return msuf, mtot, plater


def _state_to_tiles(s):
    nb = s.shape[0]
    return s.reshape(nb, NHP, 2, HD, HD).transpose(0, 1, 3, 2, 4).reshape(nb, NHP, HD, LANES)


def _tiles_to_state(s):
    nb = s.shape[0]
    return s.reshape(nb, NHP, HD, 2, HD).transpose(0, 1, 3, 2, 4).reshape(nb, NH, HD, HD)


def _layer_weights(l, p):
    w_in = p["w_in"][l]
    row = lambda a: a[l].reshape(1, -1)
    wlo = jnp.zeros((LANES, 2 * WIDTH), F32)
    wlo = wlo.at[0:DECAY_LORA, 0:WIDTH].set(p["rwkv_w_up"][l])
    wlo = wlo.at[DECAY_LORA:, WIDTH:].set(p["rwkv_a_up"][l])
    return dict(
        g_mix_pre=row(p["norm_mix_pre"]), g_mix_post=row(p["norm_mix_post"]),
        g_mlp_pre=row(p["norm_mlp_pre"]), g_mlp_post=row(p["norm_mlp_post"]),
        wm=jnp.concatenate([w_in[:, 0:OFF_F], w_in[:, OFF_C:OFF_G]], axis=1).astype(BF16),
        wf=jnp.pad(w_in[:, OFF_F:OFF_C], ((0, 0), (0, LANES - NH))).astype(BF16),
        bf=jnp.pad(p["b_forget"][l], (0, LANES - NH)).reshape(1, LANES),
        wg=w_in[:, OFF_G:].astype(BF16),
        wp=p["w_pool"][l].astype(BF16), pool_scale=row(p["pool_scale"]),
        rw=(row(p["rwkv_mu"]), wlo.astype(BF16), row(p["rwkv_w0"]), row(p["rwkv_a0"]),
            p["rwkv_g_up"][l].astype(BF16), row(p["rwkv_k_k"]), row(p["rwkv_k_a"]),
            row(p["rwkv_r_k"])),
        lnw=row(p["rwkv_ln_w"]), lnb=row(p["rwkv_ln_b"]),
        wa=p["w_branch_a"][l].astype(BF16), wb=p["w_branch_b"][l].astype(BF16),
        wc=p["w_branch_c"][l].astype(BF16), wo=p["w_out"][l].astype(BF16),
        wu=p["w_mlp_up"][l].astype(BF16), wd=p["w_mlp_down"][l].astype(BF16),
    )


def _prompt_layer(x, lw, consts):
    b, t, _ = x.shape
    n = b * t
    x2 = x.reshape(n, D_MODEL)
    u, q, k, v, kb, vb, pc, lf = _in_proj(x2, lw["g_mix_pre"], lw["wm"], lw["wf"], lw["bf"])
    r3 = lambda a: a.reshape(b, t, a.shape[-1])
    ya = _pool_prompt(r3(u), lw["wp"], lw["pool_scale"])
    cq, ck = _cumsum(r3(lf), consts["tri"])
    yb = _flash(r3(q), r3(kb), r3(vb), cq, ck, tq=512)
    rw = lw["rw"] + (consts["seg"],)
    r, w, k2, vv, kn, bv, g, bonus = _rwkv_prep_prompt(jnp.transpose(r3(pc), (1, 0, 2)), rw)
    y, s_fin = _scan(r, w, k2, vv, kn, bv, None, consts["seg2"], consts["eye2"], tc=128)
    yc = _rwkv_post(y.reshape(n, WIDTH), bonus.reshape(n, WIDTH), g.reshape(n, WIDTH),
                    lw["lnw"], lw["lnb"], consts["seg"], BF16)
    yc = jnp.transpose(yc.reshape(t, b, WIDTH), (1, 0, 2)).reshape(n, WIDTH)
    x2 = _merge(x2, ya.reshape(n, WIDTH), yb.reshape(n, WIDTH), yc, lw["g_mix_pre"],
                lw["g_mix_post"], lw["wg"], lw["wa"], lw["wb"], lw["wc"], lw["wo"])
    x2 = _mlp(x2, lw["g_mlp_pre"], lw["g_mlp_post"], lw["wu"], lw["wd"])
    new = (k.reshape(b, t, NH, HD), v.reshape(b, t, NH, HD), r3(lf)[:, :, 0:NH],
           r3(u)[:, t - POOL_PREFIX:], r3(pc)[:, t - 1], _tiles_to_state(s_fin))
    return x2.reshape(b, t, D_MODEL), new


def _sample_layer(x, lw, consts, l, cache_k, cache_v, cache_lf, state_pool, state_shift,
                  state_wkv, page_table):
    nb = x.shape[0]
    x2 = x.reshape(nb, D_MODEL)
    u, q, k, v, _, _, pc, lf = _in_proj(x2, lw["g_mix_pre"], lw["wm"], lw["wf"], lw["bf"])
    ya = _pool_sample(u, jnp.transpose(state_pool, (1, 0, 2)), lw["wp"], lw["pool_scale"])
    lfn_t = jnp.repeat(lf[:, 0:NH], PAGE_SIZE, axis=1).reshape(nb, 1, NH * PAGE_SIZE)
    bias = _decode_bias(page_table, cache_lf, lfn_t, *consts["bias_mats"])
    npages = page_table.shape[1]
    bias = bias.reshape(nb, npages, NH, PAGE_SIZE)
    yb = _decode_attn(page_table, q.reshape(nb, 1, WIDTH), k.reshape(nb, 1, WIDTH),
                      v.reshape(nb, 1, WIDTH), bias, cache_k, cache_v, l)
    rw = lw["rw"] + (consts["seg"],)
    r, w, k2, vv, kn, bv, g, bonus = _rwkv_prep_sample(pc, state_shift, rw)
    e3 = lambda a: a.reshape(1, nb, WIDTH)
    y, s_fin = _scan(e3(r), e3(w), e3(k2), e3(vv), e3(kn), e3(bv), _state_to_tiles(state_wkv),
                     consts["seg2"], consts["eye2"], tc=1)
    yc = _rwkv_post(y.reshape(nb, WIDTH), bonus, g, lw["lnw"], lw["lnb"], consts["seg"], F32)
    x2 = _merge(x2, ya, yb.reshape(nb, WIDTH), yc, lw["g_mix_pre"], lw["g_mix_post"], lw["wg"],
                lw["wa"], lw["wb"], lw["wc"], lw["wo"])
    x2 = _mlp(x2, lw["g_mlp_pre"], lw["g_mlp_post"], lw["wu"], lw["wd"])
    new = (k.reshape(nb, 1, NH, HD), v.reshape(nb, 1, NH, HD), lf[:, 0:NH].reshape(nb, 1, NH),
           jnp.concatenate([state_pool[:, 1:], u[:, None, :]], axis=1), pc,
           _tiles_to_state(s_fin))
    return x2.reshape(nb, 1, D_MODEL), new


def kernel(x_prompt, x_sample, cache_k, cache_v, cache_logf, state_pool, state_shift, state_wkv, page_table, norm_mix_pre, norm_mix_post, norm_mlp_pre, norm_mlp_post, w_in, b_forget, w_pool, pool_scale, rwkv_mu, rwkv_w0, rwkv_w_up, rwkv_a0, rwkv_a_up, rwkv_g_up, rwkv_k_k, rwkv_k_a, rwkv_r_k, rwkv_ln_w, rwkv_ln_b, w_branch_a, w_branch_b, w_branch_c, w_out, w_mlp_up, w_mlp_down):
    p = dict(norm_mix_pre=norm_mix_pre, norm_mix_post=norm_mix_post, norm_mlp_pre=norm_mlp_pre,
             norm_mlp_post=norm_mlp_post, w_in=w_in, b_forget=b_forget, w_pool=w_pool,
             pool_scale=pool_scale, rwkv_mu=rwkv_mu, rwkv_w0=rwkv_w0, rwkv_w_up=rwkv_w_up,
             rwkv_a0=rwkv_a0, rwkv_a_up=rwkv_a_up, rwkv_g_up=rwkv_g_up, rwkv_k_k=rwkv_k_k,
             rwkv_k_a=rwkv_k_a, rwkv_r_k=rwkv_r_k, rwkv_ln_w=rwkv_ln_w, rwkv_ln_b=rwkv_ln_b,
             w_branch_a=w_branch_a, w_branch_b=w_branch_b, w_branch_c=w_branch_c, w_out=w_out,
             w_mlp_up=w_mlp_up, w_mlp_down=w_mlp_down)
    depth, pool_pages = cache_k.shape[0], cache_k.shape[1]
    npages = page_table.shape[1]
    i = jnp.arange(CUM_BLK)
    eye = (jnp.arange(HD)[:, None] == (jnp.arange(LANES) % HD)[None, :]).astype(F32)
    consts = dict(seg=_seg_ones(WIDTH), seg2=_seg_ones(LANES), eye2=eye,
                  tri=(i[:, None] >= i[None, :]).astype(BF16),
                  bias_mats=_decode_bias_mats(npages))
    ck = cache_k.reshape(depth, pool_pages, PAGE_SIZE, WIDTH)
    cv = cache_v.reshape(depth, pool_pages, PAGE_SIZE, WIDTH)
    clf = cache_logf.reshape(depth, pool_pages, PAGE_SIZE * NH)
    yp, ys = x_prompt, x_sample
    outs_p, outs_s = [], []
    for l in range(depth):
        lw = _layer_weights(l, p)
        yp, new_p = _prompt_layer(yp, lw, consts)
        outs_p.append(new_p)
        ys, new_s = _sample_layer(ys, lw, consts, l, ck, cv, clf[l], state_pool[l],
                                  state_shift[l], state_wkv[l], page_table)
        outs_s.append(new_s)
    stack = lambda outs: tuple(jnp.stack([o[j] for o in outs]) for j in range(6))
    return (yp, ys) + stack(outs_p) + stack(outs_s)
```

```python
import functools

import jax
import jax.numpy as jnp
from jax import lax
from jax.experimental import pallas as pl
from jax.experimental.pallas import tpu as pltpu

F32 = jnp.float32
BF16 = jnp.bfloat16

D_MODEL = 1024
DEPTH = 4
PAGE_SIZE = 128
POOL_WINDOWS = (2, 4, 8, 16)
A_WIDTH = 512
GROUP_W = 128
POOL_PREFIX = 15
HD = 64
NH = 8
WIDTH = 512
NHP = NH // 2
LANES = 128
NEG_INF = -1e30
DECAY_LORA = 64
AAA_LORA = 64
GATE_LORA = 128
C_IN = 3 * WIDTH + DECAY_LORA + AAA_LORA + GATE_LORA
LN_X_EPS = 64e-5
D_FF = 4 * D_MODEL
RMS_EPS = 1e-6
OFF_F = 4 * WIDTH
OFF_C = OFF_F + NH
OFF_G = OFF_C + C_IN
ATT_SCALE = HD ** -0.5
VMEM_LIMIT = 56 * 1024 * 1024


def _dot(a, b):
    return jnp.dot(a, b, preferred_element_type=F32)


def _dot_nt(a, b):
    return lax.dot_general(a, b, (((1,), (1,)), ((), ())), preferred_element_type=F32)


def _split2(x):
    hi = x.astype(BF16)
    lo = (x - hi.astype(F32)).astype(BF16)
    return hi, lo


def _dot_left2(x, m):
    hi, lo = _split2(x)
    return _dot(hi, m) + _dot(lo, m)


def _dot_left3(x, m):
    hi = x.astype(BF16)
    r1 = x - hi.astype(F32)
    mid = r1.astype(BF16)
    lo = (r1 - mid.astype(F32)).astype(BF16)
    return _dot(hi, m) + _dot(mid, m) + _dot(lo, m)


def _dot_right3(m, x):
    hi = x.astype(BF16)
    r1 = x - hi.astype(F32)
    mid = r1.astype(BF16)
    lo = (r1 - mid.astype(F32)).astype(BF16)
    return _dot(m, hi) + _dot(m, mid) + _dot(m, lo)


def _rms(x, g):
    ms = jnp.mean(x * x, axis=-1, keepdims=True)
    return x * lax.rsqrt(ms + RMS_EPS) * g


def _sigmoid(x):
    return 1.0 / (1.0 + jnp.exp(-x))


def _softplus(x):
    return jnp.maximum(x, 0.0) + jnp.log1p(jnp.exp(-jnp.abs(x)))


def _params(sem):
    return pltpu.CompilerParams(dimension_semantics=sem, vmem_limit_bytes=VMEM_LIMIT)


def _const_spec(shape):
    nd = len(shape)
    return pl.BlockSpec(shape, lambda *_: (0,) * nd, pipeline_mode=pl.Buffered(1))


def _row_tile(n, pref):
    return pref if n % pref == 0 else n


def _in_proj_kernel(x_ref, g_ref, wm_ref, wf_ref, bf_ref,
                    u_ref, q_ref, k_ref, v_ref, kb_ref, vb_ref, pc_ref, lf_ref):
    h = _rms(x_ref[...], g_ref[...]).astype(BF16)
    u_ref[...] = _dot(h, wm_ref[:, 0:WIDTH])
    q_ref[...] = (_dot(h, wm_ref[:, WIDTH:2 * WIDTH]) * ATT_SCALE).astype(q_ref.dtype)
    k = _dot(h, wm_ref[:, 2 * WIDTH:3 * WIDTH])
    k_ref[...] = k
    kb_ref[...] = k.astype(BF16)
    v = _dot(h, wm_ref[:, 3 * WIDTH:4 * WIDTH])
    v_ref[...] = v
    vb_ref[...] = v.astype(BF16)
    pc_ref[...] = _dot(h, wm_ref[:, 4 * WIDTH:4 * WIDTH + C_IN])
    f = _dot(h, wf_ref[...]) + bf_ref[...]
    lf_ref[...] = -_softplus(-f)


def _in_proj(x, g, wm, wf, bf, q_dtype):
    n = x.shape[0]
    tm = _row_tile(n, 512)
    row = lambda w: pl.BlockSpec((tm, w), lambda i: (i, 0))
    out_shape = (
        jax.ShapeDtypeStruct((n, WIDTH), F32),
        jax.ShapeDtypeStruct((n, WIDTH), q_dtype),
        jax.ShapeDtypeStruct((n, WIDTH), F32),
        jax.ShapeDtypeStruct((n, WIDTH), F32),
        jax.ShapeDtypeStruct((n, WIDTH), BF16),
        jax.ShapeDtypeStruct((n, WIDTH), BF16),
        jax.ShapeDtypeStruct((n, C_IN), F32),
        jax.ShapeDtypeStruct((n, LANES), F32),
    )
    return pl.pallas_call(
        _in_proj_kernel,
        grid=(n // tm,),
        in_specs=[row(D_MODEL), _const_spec((1, D_MODEL)), _const_spec(wm.shape),
                  _const_spec(wf.shape), _const_spec((1, LANES))],
        out_specs=(row(WIDTH), row(WIDTH), row(WIDTH), row(WIDTH), row(WIDTH), row(WIDTH),
                   row(C_IN), row(LANES)),
        out_shape=out_shape,
        compiler_params=_params(("parallel",)),
        name="in_proj",
    )(x, g, wm, wf, bf)


CUM_BLK = 256


def _cumsum_kernel(lf_ref, tri_ref, c_ref, ct_ref):
    t = lf_ref.shape[1]
    carry = jnp.zeros((1, LANES), F32)
    for i in range(t // CUM_BLK):
        sl = slice(i * CUM_BLK, (i + 1) * CUM_BLK)
        c = _dot_right3(tri_ref[...], lf_ref[0, sl, :]) + carry
        c_ref[0, sl, :] = c
        ct_ref[0, :, sl] = jnp.transpose(c)[0:NH, :]
        carry = c[CUM_BLK - 1:CUM_BLK, :]


def _cumsum(lf, tri):
    b, t, _ = lf.shape
    return pl.pallas_call(
        _cumsum_kernel,
        grid=(b,),
        in_specs=[pl.BlockSpec((1, t, LANES), lambda i: (i, 0, 0)), _const_spec(tri.shape)],
        out_specs=(pl.BlockSpec((1, t, LANES), lambda i: (i, 0, 0)),
                   pl.BlockSpec((1, NH, t), lambda i: (i, 0, 0))),
        out_shape=(jax.ShapeDtypeStruct((b, t, LANES), F32),
                   jax.ShapeDtypeStruct((b, NH, t), F32)),
        compiler_params=_params(("parallel",)),
        name="fox_cumsum",
    )(lf, tri)


def _flash_kernel(q_ref, k_ref, v_ref, cq_ref, ck_ref, o_ref, m_scr, l_scr, acc_scr, *, tq, tk):
    qi = pl.program_id(1)
    ki = pl.program_id(2)

    @pl.when(ki == 0)
    def _():
        m_scr[...] = jnp.full(m_scr.shape, NEG_INF, F32)
        l_scr[...] = jnp.zeros(l_scr.shape, F32)
        acc_scr[...] = jnp.zeros(acc_scr.shape, F32)

    def block(on_diagonal):
        if on_diagonal:
            causal = (lax.broadcasted_iota(jnp.int32, (tq, tk), 1)
                      <= lax.broadcasted_iota(jnp.int32, (tq, tk), 0))
        lane = lax.broadcasted_iota(jnp.int32, (1, LANES), 1)
        cq = cq_ref[0]
        ck = ck_ref[0]
        for hp in range(NHP):
            ls = slice(hp * LANES, (hp + 1) * LANES)
            q = q_ref[0, :, ls]
            k = k_ref[0, :, ls]
            v = v_ref[0, :, ls]
            pv = []
            alpha = []
            for j in range(2):
                h = 2 * hp + j
                qm = q * ((lane // HD) == j).astype(F32).astype(BF16)
                s = _dot_nt(qm, k)
                s = s + (cq[:, h:h + 1] - ck[h:h + 1, :])
                if on_diagonal:
                    s = jnp.where(causal, s, NEG_INF)
                m_prev = m_scr[h]
                m_new = jnp.maximum(m_prev, jnp.max(s, axis=1, keepdims=True))
                a = jnp.exp(m_prev - m_new)
                p = jnp.exp(s - m_new[:, 0:1])
                l_scr[h] = a * l_scr[h] + jnp.sum(p, axis=1, keepdims=True)
                m_scr[h] = m_new
                pv.append(_dot(p.astype(BF16), v))
                alpha.append(a)
            first = (lane // HD) == 0
            acc_scr[hp] = (jnp.where(first, alpha[0], alpha[1]) * acc_scr[hp]
                           + jnp.where(first, pv[0], pv[1]))

    pl.when(ki < qi)(functools.partial(block, False))
    pl.when(ki == qi)(functools.partial(block, True))

    @pl.when(ki == qi)
    def _():
        lane = lax.broadcasted_iota(jnp.int32, (1, LANES), 1)
        for hp in range(NHP):
            l = jnp.where((lane // HD) == 0, l_scr[2 * hp], l_scr[2 * hp + 1])
            o_ref[0, :, hp * LANES:(hp + 1) * LANES] = (acc_scr[hp] / l).astype(o_ref.dtype)


def _flash(q, k, v, cq, ck, *, tq):
    b, t, _ = q.shape
    tq = min(tq, t)
    nq = t // tq
    kv_idx = lambda bi, qi, ki: (bi, jnp.minimum(ki, qi), 0)
    return pl.pallas_call(
        functools.partial(_flash_kernel, tq=tq, tk=tq),
        grid=(b, nq, nq),
        in_specs=[pl.BlockSpec((1, tq, WIDTH), lambda bi, qi, ki: (bi, qi, 0)),
                  pl.BlockSpec((1, tq, WIDTH), kv_idx),
                  pl.BlockSpec((1, tq, WIDTH), kv_idx),
                  pl.BlockSpec((1, tq, LANES), lambda bi, qi, ki: (bi, qi, 0)),
                  pl.BlockSpec((1, NH, tq), lambda bi, qi, ki: (bi, 0, jnp.minimum(ki, qi)))],
        out_specs=pl.BlockSpec((1, tq, WIDTH), lambda bi, qi, ki: (bi, qi, 0)),
        out_shape=jax.ShapeDtypeStruct((b, t, WIDTH), BF16),
        scratch_shapes=[pltpu.VMEM((NH, tq, LANES), F32), pltpu.VMEM((NH, tq, LANES), F32),
                        pltpu.VMEM((NHP, tq, LANES), F32)],
        compiler_params=_params(("parallel", "parallel", "arbitrary")),
        name="fox_flash",
    )(q, k, v, cq, ck)


def _decode_bias_kernel(pt_ref, tbl_ref, lfn_ref, msuf_ref, mtot_ref, plater_ref, o_ref, g_scr):
    b = pl.program_id(0)
    npages = g_scr.shape[0]
    for p in range(npages):
        g_scr[p] = tbl_ref[pt_ref[b, p]]
    g = g_scr[...].reshape(npages * NH, PAGE_SIZE)
    within = _dot_left3(g, msuf_ref[...])
    tot = _dot_left3(g, mtot_ref[...])
    later = _dot_right3(plater_ref[...], tot)
    o_ref[0] = (within + later).reshape(npages, NH, PAGE_SIZE) + lfn_ref[...]


def _decode_bias(page_table, tbl, lfn, msuf, mtot, plater):
    nb, npages = page_table.shape
    grid_spec = pltpu.PrefetchScalarGridSpec(
        num_scalar_prefetch=1,
        grid=(nb,),
        in_specs=[_const_spec(tbl.shape),
                  pl.BlockSpec((1, NH, PAGE_SIZE), lambda b, pt: (b, 0, 0)),
                  _const_spec(msuf.shape), _const_spec(mtot.shape), _const_spec(plater.shape)],
        out_specs=pl.BlockSpec((1, npages, NH, PAGE_SIZE), lambda b, pt: (b, 0, 0, 0)),
        scratch_shapes=[pltpu.VMEM((npages, NH, PAGE_SIZE), F32)],
    )
    return pl.pallas_call(
        _decode_bias_kernel,
        grid_spec=grid_spec,
        out_shape=jax.ShapeDtypeStruct((nb, npages, NH, PAGE_SIZE), F32),
        compiler_params=_params(("arbitrary",)),
        name="fox_decode_bias",
    )(page_table, tbl, lfn, msuf, mtot, plater)


PAGES_PER_STEP = 8


def _col_bcast(row_pair, eye2, j):
    lane = lax.broadcasted_iota(jnp.int32, (1, LANES), 1)
    diag = jnp.where((lane // HD) == j, jnp.broadcast_to(row_pair, (HD, LANES)) * eye2, 0.0)
    return jnp.broadcast_to(jnp.sum(diag, axis=1, keepdims=True), (HD, LANES))


def _decode_attn_kernel(pt_ref, q_ref, kn_ref, vn_ref, bias_ref, eye_ref, *refs):
    k_refs = refs[0:PAGES_PER_STEP]
    v_refs = refs[PAGES_PER_STEP:2 * PAGES_PER_STEP]
    o_ref, m_scr, l_scr, acc_scr, qc_scr = refs[2 * PAGES_PER_STEP:]
    g = pl.program_id(1)
    eye2 = eye_ref[...]
    lane = lax.broadcasted_iota(jnp.int32, (1, LANES), 1)
    sub = lax.broadcasted_iota(jnp.int32, (NH, PAGE_SIZE), 0)

    @pl.when(g == 0)
    def _():
        q = q_ref[0]
        own = (lax.broadcasted_iota(jnp.int32, (NH, WIDTH), 1) // HD
               == lax.broadcasted_iota(jnp.int32, (NH, WIDTH), 0))
        qk = jnp.broadcast_to(q * kn_ref[0], (NH, WIDTH))
        s_new = jnp.sum(jnp.where(own, qk, 0.0), axis=1, keepdims=True)
        m_scr[...] = jnp.broadcast_to(s_new, m_scr.shape)
        l_scr[...] = jnp.ones(l_scr.shape, F32)
        vn = vn_ref[0]
        for h in range(NH):
            ls = slice((h // 2) * LANES, (h // 2 + 1) * LANES)
            qc_scr[h] = _col_bcast(q[:, ls], eye2, h % 2)
            acc_scr[h] = jnp.where(lane == 0, _col_bcast(vn[:, ls], eye2, h % 2), 0.0)

    s_parts = []
    for j in range(PAGES_PER_STEP):
        s_page = jnp.zeros((NH, PAGE_SIZE), F32)
        for h in range(NH):
            sh = jnp.sum(k_refs[j][0, 0, h] * qc_scr[h], axis=0, keepdims=True)
            s_page = jnp.where(sub == h, jnp.broadcast_to(sh, (NH, PAGE_SIZE)), s_page)
        s_parts.append(s_page + bias_ref[0, j])
    s = jnp.concatenate(s_parts, axis=1)
    m_prev = m_scr[...]
    m_new = jnp.maximum(m_prev, jnp.max(s, axis=1, keepdims=True))
    a = jnp.exp(m_prev - m_new)
    p = jnp.exp(s - m_new[:, 0:1])
    l_scr[...] = a * l_scr[...] + jnp.sum(p, axis=1, keepdims=True)
    m_scr[...] = m_new
    for h in range(NH):
        acc = acc_scr[h] * a[h:h + 1, 0:1]
        for j in range(PAGES_PER_STEP):
            ph = p[h:h + 1, j * PAGE_SIZE:(j + 1) * PAGE_SIZE]
            acc = acc + v_refs[j][0, 0, h] * jnp.broadcast_to(ph, (HD, PAGE_SIZE))
        acc_scr[h] = acc

    @pl.when(g == pl.num_programs(1) - 1)
    def _():
        rows = []
        for hp in range(NHP):
            pair = []
            for j in range(2):
                h = 2 * hp + j
                col = jnp.sum(acc_scr[h], axis=1, keepdims=True) / l_scr[h:h + 1, 0:1]
                pair.append(jnp.sum(jnp.broadcast_to(col, (HD, LANES)) * eye2, axis=0,
                                    keepdims=True))
            rows.append(jnp.where((lane // HD) == 0, pair[0], pair[1]))
        o_ref[0] = jnp.concatenate(rows, axis=1)


def _decode_attn(page_table, q, k_new, v_new, bias, eye2, cache_kt, cache_vt, layer):
    nb, npages = page_table.shape
    ng = npages // PAGES_PER_STEP

    def page_spec(j):
        return pl.BlockSpec((1, 1, NH, HD, PAGE_SIZE),
                            lambda b, g, pt: (layer, pt[b, g * PAGES_PER_STEP + j], 0, 0, 0))

    vec = lambda: pl.BlockSpec((1, 1, WIDTH), lambda b, g, pt: (b, 0, 0))
    grid_spec = pltpu.PrefetchScalarGridSpec(
        num_scalar_prefetch=1,
        grid=(nb, ng),
        in_specs=[vec(), vec(), vec(),
                  pl.BlockSpec((1, PAGES_PER_STEP, NH, PAGE_SIZE), lambda b, g, pt: (b, g, 0, 0)),
                  _const_spec(eye2.shape)]
                 + [page_spec(j) for j in range(PAGES_PER_STEP)]
                 + [page_spec(j) for j in range(PAGES_PER_STEP)],
        out_specs=pl.BlockSpec((1, 1, WIDTH), lambda b, g, pt: (b, 0, 0)),
        scratch_shapes=[pltpu.VMEM((NH, LANES), F32), pltpu.VMEM((NH, LANES), F32),
                        pltpu.VMEM((NH, HD, PAGE_SIZE), F32), pltpu.VMEM((NH, HD, LANES), F32)],
    )
    return pl.pallas_call(
        _decode_attn_kernel,
        grid_spec=grid_spec,
        out_shape=jax.ShapeDtypeStruct((nb, 1, WIDTH), F32),
        compiler_params=_params(("parallel", "arbitrary")),
        name="fox_decode_attn",
    )(page_table, q, k_new, v_new, bias, eye2,
      *([cache_kt] * PAGES_PER_STEP), *([cache_vt] * PAGES_PER_STEP))


POOL_HALO = 16


def _pool_project(pooled, wp_ref, sc_ref):
    outs = []
    for g in range(len(POOL_WINDOWS)):
        ls = slice(g * GROUP_W, (g + 1) * GROUP_W)
        outs.append(_dot(pooled[g].astype(BF16), wp_ref[g]) * sc_ref[:, ls])
    return outs


def _pool_prompt_kernel(u_ref, wp_ref, sc_ref, o_ref, ext_scr, *, tt):
    ti = pl.program_id(1)

    @pl.when(ti == 0)
    def _():
        ext_scr[0:POOL_HALO, :] = jnp.zeros((POOL_HALO, A_WIDTH), F32)

    ext_scr[POOL_HALO:POOL_HALO + tt, :] = u_ref[0]
    pos1 = ti * tt + lax.broadcasted_iota(jnp.int32, (tt, GROUP_W), 0) + 1
    pooled = []
    for g, w in enumerate(POOL_WINDOWS):
        ls = slice(g * GROUP_W, (g + 1) * GROUP_W)
        x = ext_scr[POOL_HALO:POOL_HALO + tt, ls]
        win = x
        for d in range(1, w):
            win = win + ext_scr[POOL_HALO - d:POOL_HALO - d + tt, ls]
        cnt = jnp.minimum(w, pos1).astype(F32)
        pooled.append(win / cnt - x)
    outs = _pool_project(pooled, wp_ref, sc_ref)
    for g in range(len(POOL_WINDOWS)):
        o_ref[0, :, g * GROUP_W:(g + 1) * GROUP_W] = outs[g].astype(o_ref.dtype)
    ext_scr[0:POOL_HALO, :] = ext_scr[tt:tt + POOL_HALO, :]


def _pool_prompt(u, wp, sc):
    b, t, _ = u.shape
    tt = _row_tile(t, 512)
    return pl.pallas_call(
        functools.partial(_pool_prompt_kernel, tt=tt),
        grid=(b, t // tt),
        in_specs=[pl.BlockSpec((1, tt, A_WIDTH), lambda bi, ti: (bi, ti, 0)),
                  _const_spec(wp.shape), _const_spec((1, A_WIDTH))],
        out_specs=pl.BlockSpec((1, tt, A_WIDTH), lambda bi, ti: (bi, ti, 0)),
        out_shape=jax.ShapeDtypeStruct((b, t, A_WIDTH), BF16),
        scratch_shapes=[pltpu.VMEM((POOL_HALO + tt, A_WIDTH), F32)],
        compiler_params=_params(("parallel", "arbitrary")),
        name="pool_prompt",
    )(u, wp, sc)


def _pool_sample_kernel(u_ref, prev_ref, wp_ref, sc_ref, o_ref):
    pooled = []
    for g, w in enumerate(POOL_WINDOWS):
        ls = slice(g * GROUP_W, (g + 1) * GROUP_W)
        x = u_ref[:, ls]
        win = x
        for d in range(1, w):
            win = win + prev_ref[POOL_PREFIX - d, :, ls]
        pooled.append(win / float(w) - x)
    outs = _pool_project(pooled, wp_ref, sc_ref)
    for g in range(len(POOL_WINDOWS)):
        o_ref[:, g * GROUP_W:(g + 1) * GROUP_W] = outs[g]


def _pool_sample(u, prev_t, wp, sc):
    nb = u.shape[0]
    return pl.pallas_call(
        _pool_sample_kernel,
        out_shape=jax.ShapeDtypeStruct((nb, A_WIDTH), F32),
        compiler_params=pltpu.CompilerParams(vmem_limit_bytes=VMEM_LIMIT),
        name="pool_sample",
    )(u, prev_t, wp, sc)


SCAN_BB = 8


def _rwkv_prep_math(pc, prev, mu_ref, wlo_ref, w0_ref, a0_ref, gup_ref, kk_ref, ka_ref, rk_ref,
                    seg_ref):
    xs = pc + (prev - pc) * mu_ref[...]
    r = xs[:, 0:WIDTH]
    k = xs[:, WIDTH:2 * WIDTH]
    v = xs[:, 2 * WIDTH:3 * WIDTH]
    lora = xs[:, 3 * WIDTH:3 * WIDTH + LANES]
    gd = xs[:, 3 * WIDTH + LANES:3 * WIDTH + 2 * LANES]
    lane = lax.broadcasted_iota(jnp.int32, (1, LANES), 1)
    lhs = jnp.where(lane < DECAY_LORA, jnp.tanh(lora), lora).astype(BF16)
    up = _dot(lhs, wlo_ref[...])
    w = -_softplus(-(w0_ref[...] + up[:, 0:WIDTH])) - 0.5
    decay = jnp.exp(-jnp.exp(w))
    a = _sigmoid(a0_ref[...] + up[:, WIDTH:2 * WIDTH])
    g = _dot(_sigmoid(gd).astype(BF16), gup_ref[...])
    kk = k * kk_ref[...]
    nrm = jnp.sqrt(_dot_left2(kk * kk, seg_ref[...]))
    kk = kk / jnp.maximum(nrm, 1e-12)
    k2 = k * (1.0 + (a - 1.0) * ka_ref[...])
    bonus = _dot_left2(r * k2 * rk_ref[...], seg_ref[...]) * v
    return r, decay, k2, v, -kk, kk * a, g, bonus


def _rwkv_prep_prompt_kernel(pc_ref, mu_ref, wlo_ref, w0_ref, a0_ref, gup_ref, kk_ref, ka_ref,
                             rk_ref, seg_ref, *refs, tt):
    out_refs, last_scr = refs[:-1], refs[-1]
    ti = pl.program_id(0)

    @pl.when(ti == 0)
    def _():
        last_scr[...] = jnp.zeros(last_scr.shape, F32)

    pc3 = pc_ref[...]
    prev3 = jnp.concatenate([last_scr[...], pc3[0:tt - 1]], axis=0)
    last_scr[...] = pc3[tt - 1:tt]
    rows = tt * SCAN_BB
    outs = _rwkv_prep_math(pc3.reshape(rows, C_IN), prev3.reshape(rows, C_IN), mu_ref, wlo_ref,
                           w0_ref, a0_ref, gup_ref, kk_ref, ka_ref, rk_ref, seg_ref)
    for o_ref, val in zip(out_refs, outs):
        o_ref[...] = val.reshape(tt, SCAN_BB, WIDTH)


def _rwkv_prep_prompt(pc_tm, weights):
    t, b, _ = pc_tm.shape
    assert b == SCAN_BB
    tt = _row_tile(t, 64)
    tile = lambda w: pl.BlockSpec((tt, b, w), lambda ti: (ti, 0, 0))
    return pl.pallas_call(
        functools.partial(_rwkv_prep_prompt_kernel, tt=tt),
        grid=(t // tt,),
        in_specs=[tile(C_IN)] + [_const_spec(w.shape) for w in weights],
        out_specs=tuple(tile(WIDTH) for _ in range(8)),
        out_shape=tuple(jax.ShapeDtypeStruct((t, b, WIDTH), F32) for _ in range(8)),
        scratch_shapes=[pltpu.VMEM((1, b, C_IN), F32)],
        compiler_params=_params(("arbitrary",)),
        name="rwkv_prep_prompt",
    )(pc_tm, *weights)


def _rwkv_prep_sample_kernel(pc_ref, prev_ref, mu_ref, wlo_ref, w0_ref, a0_ref, gup_ref, kk_ref,
                             ka_ref, rk_ref, seg_ref, *out_refs):
    outs = _rwkv_prep_math(pc_ref[...], prev_ref[...], mu_ref, wlo_ref, w0_ref, a0_ref, gup_ref,
                           kk_ref, ka_ref, rk_ref, seg_ref)
    for o_ref, val in zip(out_refs, outs):
        o_ref[...] = val


def _rwkv_prep_sample(pc, prev, weights):
    nb = pc.shape[0]
    return pl.pallas_call(
        _rwkv_prep_sample_kernel,
        out_shape=tuple(jax.ShapeDtypeStruct((nb, WIDTH), F32) for _ in range(8)),
        compiler_params=pltpu.CompilerParams(vmem_limit_bytes=VMEM_LIMIT),
        name="rwkv_prep_sample",
    )(pc, prev, *weights)


def _scan_kernel(r_ref, w_ref, k_ref, v_ref, kn_ref, b_ref, s0_ref, seg_ref, eye_ref,
                 y_ref, so_ref, s_scr, *, tc, zero_init):
    ci = pl.program_id(1)

    @pl.when(ci == 0)
    def _():
        if zero_init:
            s_scr[...] = jnp.zeros(s_scr.shape, F32)
        else:
            s_scr[...] = s0_ref[...]

    seg = seg_ref[...]
    eye2 = eye_ref[...]
    groups = [(b, hp) for b in range(SCAN_BB) for hp in range(NHP)]
    ng = len(groups)
    sub = lax.broadcasted_iota(jnp.int32, (SCAN_BB, LANES), 0)

    def row(slab, b, hp):
        return jnp.broadcast_to(slab[b:b + 1, hp * LANES:(hp + 1) * LANES], (HD, LANES))

    def seg_sum(parts):
        return _dot(jnp.concatenate(parts, axis=0).astype(BF16), seg)

    def write_y(yb, t):
        y_tiles = [jnp.zeros((SCAN_BB, LANES), F32) for _ in range(NHP)]
        for n, (b, hp) in enumerate(groups):
            y_row = jnp.sum(yb[n * HD:(n + 1) * HD] * eye2, axis=0, keepdims=True)
            y_tiles[hp] = jnp.where(sub == b, jnp.broadcast_to(y_row, (SCAN_BB, LANES)),
                                    y_tiles[hp])
        y_ref[t] = jnp.concatenate(y_tiles, axis=1)

    def step(t, carry):
        tp = jnp.maximum(t - 1, 0)
        w_t, k_t, v_t, kn_t, b_t = (ref[t] for ref in (w_ref, k_ref, v_ref, kn_ref, b_ref))
        r_p = r_ref[tp]
        res = seg_sum([s_scr[b, hp] * row(kn_t, b, hp) for b, hp in groups]
                      + [eye2 * row(v_t, b, hp) for b, hp in groups]
                      + [s_scr[b, hp] * row(r_p, b, hp) for b, hp in groups])
        for n, (b, hp) in enumerate(groups):
            sa = res[n * HD:(n + 1) * HD]
            vb = res[(ng + n) * HD:(ng + n + 1) * HD]
            s_scr[b, hp] = (s_scr[b, hp] * row(w_t, b, hp) + sa * row(b_t, b, hp)
                            + vb * row(k_t, b, hp))
        write_y(res[2 * ng * HD:], tp)
        return carry

    lax.fori_loop(0, tc, step, 0)
    r_last = r_ref[tc - 1]
    write_y(seg_sum([s_scr[b, hp] * row(r_last, b, hp) for b, hp in groups]), tc - 1)

    @pl.when(ci == pl.num_programs(1) - 1)
    def _():
        so_ref[...] = s_scr[...]


def _scan(r, w, k, v, kn, bv, s0, seg2, eye2, *, tc):
    t, nb, _ = r.shape
    tc = min(tc, t)
    zero_init = s0 is None
    if zero_init:
        s0 = jnp.zeros((SCAN_BB, NHP, HD, LANES), F32)
        s0_spec = pl.BlockSpec((SCAN_BB, NHP, HD, LANES), lambda bi, ci: (0, 0, 0, 0))
    else:
        s0_spec = pl.BlockSpec((SCAN_BB, NHP, HD, LANES), lambda bi, ci: (bi, 0, 0, 0))
    seq = lambda: pl.BlockSpec((tc, SCAN_BB, WIDTH), lambda bi, ci: (ci, bi, 0))
    return pl.pallas_call(
        functools.partial(_scan_kernel, tc=tc, zero_init=zero_init),
        grid=(nb // SCAN_BB, t // tc),
        in_specs=[seq(), seq(), seq(), seq(), seq(), seq(), s0_spec,
                  _const_spec(seg2.shape), _const_spec(eye2.shape)],
        out_specs=(seq(), pl.BlockSpec((SCAN_BB, NHP, HD, LANES), lambda bi, ci: (bi, 0, 0, 0))),
        out_shape=(jax.ShapeDtypeStruct((t, nb, WIDTH), F32),
                   jax.ShapeDtypeStruct((nb, NHP, HD, LANES), F32)),
        scratch_shapes=[pltpu.VMEM((SCAN_BB, NHP, HD, LANES), F32)],
        compiler_params=_params(("parallel", "arbitrary")),
        name="rwkv_scan",
    )(r, w, k, v, kn, bv, s0, seg2, eye2)


def _rwkv_post_kernel(y_ref, bonus_ref, g_ref, lnw_ref, lnb_ref, seg_ref, o_ref):
    y = y_ref[...]
    seg = seg_ref[...]
    mean = _dot_left2(y, seg) * (1.0 / HD)
    d = y - mean
    var = _dot_left2(d * d, seg) * (1.0 / HD)
    yn = d * lax.rsqrt(var + LN_X_EPS) * lnw_ref[...] + lnb_ref[...]
    o_ref[...] = ((yn + bonus_ref[...]) * g_ref[...]).astype(o_ref.dtype)


def _rwkv_post(y, bonus, g, lnw, lnb, seg, out_dtype):
    n = y.shape[0]
    tm = _row_tile(n, 512)
    row = pl.BlockSpec((tm, WIDTH), lambda i: (i, 0))
    return pl.pallas_call(
        _rwkv_post_kernel,
        grid=(n // tm,),
        in_specs=[row, row, row, _const_spec((1, WIDTH)), _const_spec((1, WIDTH)),
                  _const_spec(seg.shape)],
        out_specs=row,
        out_shape=jax.ShapeDtypeStruct((n, WIDTH), out_dtype),
        compiler_params=_params(("parallel",)),
        name="rwkv_post",
    )(y, bonus, g, lnw, lnb, seg)


def _merge_kernel(x_ref, ya_ref, yb_ref, yc_ref, gpre_ref, gpost_ref, wg_ref, wa_ref, wb_ref,
                  wc_ref, wo_ref, o_ref):
    x = x_ref[...]
    h = _rms(x, gpre_ref[...]).astype(BF16)
    merged = None
    for n, (y_ref, w_ref) in enumerate(((ya_ref, wa_ref), (yb_ref, wb_ref), (yc_ref, wc_ref))):
        gate = _sigmoid(_dot(h, wg_ref[:, n * D_MODEL:(n + 1) * D_MODEL]))
        term = gate * _dot(y_ref[...].astype(BF16), w_ref[...])
        merged = term if merged is None else merged + term
    o = _dot(merged.astype(BF16), wo_ref[...])
    o_ref[...] = x + _rms(o, gpost_ref[...])


def _merge(x, ya, yb, yc, gpre, gpost, wg, wa, wb, wc, wo):
    n = x.shape[0]
    tm = _row_tile(n, 512)
    row = lambda w: pl.BlockSpec((tm, w), lambda i: (i, 0))
    return pl.pallas_call(
        _merge_kernel,
        grid=(n // tm,),
        in_specs=[row(D_MODEL), row(WIDTH), row(WIDTH), row(WIDTH),
                  _const_spec((1, D_MODEL)), _const_spec((1, D_MODEL)),
                  _const_spec(wg.shape), _const_spec(wa.shape), _const_spec(wb.shape),
                  _const_spec(wc.shape), _const_spec(wo.shape)],
        out_specs=row(D_MODEL),
        out_shape=jax.ShapeDtypeStruct((n, D_MODEL), F32),
        compiler_params=_params(("parallel",)),
        name="merge",
    )(x, ya, yb, yc, gpre, gpost, wg, wa, wb, wc, wo)


def _mlp_kernel(x_ref, gpre_ref, gpost_ref, wu_ref, wd_ref, o_ref):
    x = x_ref[...]
    h = _rms(x, gpre_ref[...]).astype(BF16)
    up = jnp.maximum(_dot(h, wu_ref[...]), 0.0)
    down = _dot((up * up).astype(BF16), wd_ref[...])
    o_ref[...] = x + _rms(down, gpost_ref[...])


def _mlp(x, gpre, gpost, wu, wd):
    n = x.shape[0]
    tm = _row_tile(n, 512)
    row = pl.BlockSpec((tm, D_MODEL), lambda i: (i, 0))
    return pl.pallas_call(
        _mlp_kernel,
        grid=(n // tm,),
        in_specs=[row, _const_spec((1, D_MODEL)), _const_spec((1, D_MODEL)),
                  _const_spec(wu.shape), _const_spec(wd.shape)],
        out_specs=row,
        out_shape=jax.ShapeDtypeStruct((n, D_MODEL), F32),
        compiler_params=_params(("parallel",)),
        name="mlp",
    )(x, gpre, gpost, wu, wd)


def _seg_ones(n):
    i = jnp.arange(n)
    return (i[:, None] // HD == i[None, :] // HD).astype(BF16)


def _decode_bias_mats(npages):
    pos = jnp.arange(PAGE_SIZE)
    msuf = (pos[:, None] > pos[None, :]).astype(BF16)
    mtot = jnp.ones((PAGE_SIZE, PAGE_SIZE), BF16)
    r = jnp.arange(npages * NH)
    plater = ((r[None, :] // NH > r[:, None] // NH)
              & (r[None, :] % NH == r[:, None] % NH)).astype(BF16)
    return msuf, mtot, plater


def _state_to_tiles(s):
    nb = s.shape[0]
    return s.reshape(nb, NHP, 2, HD, HD).transpose(0, 1, 3, 2, 4).reshape(nb, NHP, HD, LANES)


def _tiles_to_state(s):
    nb = s.shape[0]
    return s.reshape(nb, NHP, HD, 2, HD).transpose(0, 1, 3, 2, 4).reshape(nb, NH, HD, HD)


def _layer_weights(l, p):
    w_in = p["w_in"][l]
    row = lambda a: a[l].reshape(1, -1)
    wlo = jnp.zeros((LANES, 2 * WIDTH), F32)
    wlo = wlo.at[0:DECAY_LORA, 0:WIDTH].set(p["rwkv_w_up"][l])
    wlo = wlo.at[DECAY_LORA:, WIDTH:].set(p["rwkv_a_up"][l])
    return dict(
        g_mix_pre=row(p["norm_mix_pre"]), g_mix_post=row(p["norm_mix_post"]),
        g_mlp_pre=row(p["norm_mlp_pre"]), g_mlp_post=row(p["norm_mlp_post"]),
        wm=jnp.concatenate([w_in[:, 0:OFF_F], w_in[:, OFF_C:OFF_G]], axis=1).astype(BF16),
        wf=jnp.pad(w_in[:, OFF_F:OFF_C], ((0, 0), (0, LANES - NH))).astype(BF16),
        bf=jnp.pad(p["b_forget"][l], (0, LANES - NH)).reshape(1, LANES),
        wg=w_in[:, OFF_G:].astype(BF16),
        wp=p["w_pool"][l].astype(BF16), pool_scale=row(p["pool_scale"]),
        rw=(row(p["rwkv_mu"]), wlo.astype(BF16), row(p["rwkv_w0"]), row(p["rwkv_a0"]),
            p["rwkv_g_up"][l].astype(BF16), row(p["rwkv_k_k"]), row(p["rwkv_k_a"]),
            row(p["rwkv_r_k"])),
        lnw=row(p["rwkv_ln_w"]), lnb=row(p["rwkv_ln_b"]),
        wa=p["w_branch_a"][l].astype(BF16), wb=p["w_branch_b"][l].astype(BF16),
        wc=p["w_branch_c"][l].astype(BF16), wo=p["w_out"][l].astype(BF16),
        wu=p["w_mlp_up"][l].astype(BF16), wd=p["w_mlp_down"][l].astype(BF16),
    )


def _prompt_layer(x, lw, consts):
    b, t, _ = x.shape
    n = b * t
    x2 = x.reshape(n, D_MODEL)
    u, q, k, v, kb, vb, pc, lf = _in_proj(x2, lw["g_mix_pre"], lw["wm"], lw["wf"], lw["bf"], BF16)
    r3 = lambda a: a.reshape(b, t, a.shape[-1])
    ya = _pool_prompt(r3(u), lw["wp"], lw["pool_scale"])
    cq, ck = _cumsum(r3(lf), consts["tri"])
    yb = _flash(r3(q), r3(kb), r3(vb), cq, ck, tq=512)
    rw = lw["rw"] + (consts["seg"],)
    r, w, k2, vv, kn, bv, g, bonus = _rwkv_prep_prompt(jnp.transpose(r3(pc), (1, 0, 2)), rw)
    y, s_fin = _scan(r, w, k2, vv, kn, bv, None, consts["seg2"], consts["eye2"], tc=128)
    yc = _rwkv_post(y.reshape(n, WIDTH), bonus.reshape(n, WIDTH), g.reshape(n, WIDTH),
                    lw["lnw"], lw["lnb"], consts["seg"], BF16)
    yc = jnp.transpose(yc.reshape(t, b, WIDTH), (1, 0, 2)).reshape(n, WIDTH)
    x2 = _merge(x2, ya.reshape(n, WIDTH), yb.reshape(n, WIDTH), yc, lw["g_mix_pre"],
                lw["g_mix_post"], lw["wg"], lw["wa"], lw["wb"], lw["wc"], lw["wo"])
    x2 = _mlp(x2, lw["g_mlp_pre"], lw["g_mlp_post"], lw["wu"], lw["wd"])
    new = (k.reshape(b, t, NH, HD), v.reshape(b, t, NH, HD), r3(lf)[:, :, 0:NH],
           r3(u)[:, t - POOL_PREFIX:], r3(pc)[:, t - 1], _tiles_to_state(s_fin))
    return x2.reshape(b, t, D_MODEL), new


def _sample_layer(x, lw, consts, l, cache_k, cache_v, cache_lf, state_pool, state_shift,
                  state_wkv, page_table):
    nb = x.shape[0]
    x2 = x.reshape(nb, D_MODEL)
    u, q, k, v, _, _, pc, lf = _in_proj(x2, lw["g_mix_pre"], lw["wm"], lw["wf"], lw["bf"], F32)
    ya = _pool_sample(u, jnp.transpose(state_pool, (1, 0, 2)), lw["wp"], lw["pool_scale"])
    lfn = jnp.broadcast_to(lf[:, 0:NH, None], (nb, NH, PAGE_SIZE))
    bias = _decode_bias(page_table, cache_lf, lfn, *consts["bias_mats"])
    yb = _decode_attn(page_table, q.reshape(nb, 1, WIDTH), k.reshape(nb, 1, WIDTH),
                      v.reshape(nb, 1, WIDTH), bias, consts["eye2"], cache_k, cache_v, l)
    rw = lw["rw"] + (consts["seg"],)
    r, w, k2, vv, kn, bv, g, bonus = _rwkv_prep_sample(pc, state_shift, rw)
    e3 = lambda a: a.reshape(1, nb, WIDTH)
    y, s_fin = _scan(e3(r), e3(w), e3(k2), e3(vv), e3(kn), e3(bv), _state_to_tiles(state_wkv),
                     consts["seg2"], consts["eye2"], tc=1)
    yc = _rwkv_post(y.reshape(nb, WIDTH), bonus, g, lw["lnw"], lw["lnb"], consts["seg"], F32)
    x2 = _merge(x2, ya, yb.reshape(nb, WIDTH), yc, lw["g_mix_pre"], lw["g_mix_post"], lw["wg"],
                lw["wa"], lw["wb"], lw["wc"], lw["wo"])
    x2 = _mlp(x2, lw["g_mlp_pre"], lw["g_mlp_post"], lw["wu"], lw["wd"])
    new = (k.reshape(nb, 1, NH, HD), v.reshape(nb, 1, NH, HD), lf[:, 0:NH].reshape(nb, 1, NH),
           jnp.concatenate([state_pool[:, 1:], u[:, None, :]], axis=1), pc,
           _tiles_to_state(s_fin))
    return x2.reshape(nb, 1, D_MODEL), new


def kernel(x_prompt, x_sample, cache_k, cache_v, cache_logf, state_pool, state_shift, state_wkv, page_table, norm_mix_pre, norm_mix_post, norm_mlp_pre, norm_mlp_post, w_in, b_forget, w_pool, pool_scale, rwkv_mu, rwkv_w0, rwkv_w_up, rwkv_a0, rwkv_a_up, rwkv_g_up, rwkv_k_k, rwkv_k_a, rwkv_r_k, rwkv_ln_w, rwkv_ln_b, w_branch_a, w_branch_b, w_branch_c, w_out, w_mlp_up, w_mlp_down):
    p = dict(norm_mix_pre=norm_mix_pre, norm_mix_post=norm_mix_post, norm_mlp_pre=norm_mlp_pre,
             norm_mlp_post=norm_mlp_post, w_in=w_in, b_forget=b_forget, w_pool=w_pool,
             pool_scale=pool_scale, rwkv_mu=rwkv_mu, rwkv_w0=rwkv_w0, rwkv_w_up=rwkv_w_up,
             rwkv_a0=rwkv_a0, rwkv_a_up=rwkv_a_up, rwkv_g_up=rwkv_g_up, rwkv_k_k=rwkv_k_k,
             rwkv_k_a=rwkv_k_a, rwkv_r_k=rwkv_r_k, rwkv_ln_w=rwkv_ln_w, rwkv_ln_b=rwkv_ln_b,
             w_branch_a=w_branch_a, w_branch_b=w_branch_b, w_branch_c=w_branch_c, w_out=w_out,
             w_mlp_up=w_mlp_up, w_mlp_down=w_mlp_down)
    depth, pool_pages = cache_k.shape[0], cache_k.shape[1]
    npages = page_table.shape[1]
    i = jnp.arange(CUM_BLK)
    eye = (jnp.arange(HD)[:, None] == (jnp.arange(LANES) % HD)[None, :]).astype(F32)
    consts = dict(seg=_seg_ones(WIDTH), seg2=_seg_ones(LANES), eye2=eye,
                  tri=(i[:, None] >= i[None, :]).astype(BF16),
                  bias_mats=_decode_bias_mats(npages))
    ck = jnp.transpose(cache_k, (0, 1, 3, 4, 2))
    cv = jnp.transpose(cache_v, (0, 1, 3, 4, 2))
    clf = jnp.transpose(cache_logf, (0, 1, 3, 2))
    yp, ys = x_prompt, x_sample
    outs_p, outs_s = [], []
    for l in range(depth):
        lw = _layer_weights(l, p)
        yp, new_p = _prompt_layer(yp, lw, consts)
        outs_p.append(new_p)
        ys, new_s = _sample_layer(ys, lw, consts, l, ck, cv, clf[l], state_pool[l],
                                  state_shift[l], state_wkv[l], page_table)
        outs_s.append(new_s)
    stack = lambda outs: tuple(jnp.stack([o[j] for o in outs]) for j in range(6))
    return (yp, ys) + stack(outs_p) + stack(outs_s)
```

```python
import functools

import jax
import jax.numpy as jnp
from jax import lax
from jax.experimental import pallas as pl
from jax.experimental.pallas import tpu as pltpu

F32 = jnp.float32
BF16 = jnp.bfloat16

D_MODEL = 1024
DEPTH = 4
PAGE_SIZE = 128
POOL_WINDOWS = (2, 4, 8, 16)
A_WIDTH = 512
GROUP_W = 128
POOL_PREFIX = 15
HD = 64
NH = 8
WIDTH = 512
NHP = NH // 2
LANES = 128
NEG_INF = -1e30
DECAY_LORA = 64
AAA_LORA = 64
GATE_LORA = 128
C_IN = 3 * WIDTH + DECAY_LORA + AAA_LORA + GATE_LORA
LN_X_EPS = 64e-5
D_FF = 4 * D_MODEL
RMS_EPS = 1e-6
OFF_F = 4 * WIDTH
OFF_C = OFF_F + NH
OFF_G = OFF_C + C_IN
ATT_SCALE = HD ** -0.5
VMEM_LIMIT = 56 * 1024 * 1024


def _dot(a, b):
    return jnp.dot(a, b, preferred_element_type=F32)


def _dot_nt(a, b):
    return lax.dot_general(a, b, (((1,), (1,)), ((), ())), preferred_element_type=F32)


def _split2(x):
    hi = x.astype(BF16)
    lo = (x - hi.astype(F32)).astype(BF16)
    return hi, lo


def _dot_left2(x, m):
    hi, lo = _split2(x)
    return _dot(hi, m) + _dot(lo, m)


def _split3(x):
    hi = x.astype(BF16)
    r1 = x - hi.astype(F32)
    mid = r1.astype(BF16)
    lo = (r1 - mid.astype(F32)).astype(BF16)
    return hi, mid, lo


def _dot_left3(x, m):
    hi, mid, lo = _split3(x)
    return _dot(hi, m) + _dot(mid, m) + _dot(lo, m)


def _dot_right3(m, x):
    hi, mid, lo = _split3(x)
    return _dot(m, hi) + _dot(m, mid) + _dot(m, lo)


def _rms(x, g):
    ms = jnp.mean(x * x, axis=-1, keepdims=True)
    return x * lax.rsqrt(ms + RMS_EPS) * g


def _sigmoid(x):
    return 1.0 / (1.0 + jnp.exp(-x))


def _softplus(x):
    return jnp.maximum(x, 0.0) + jnp.log1p(jnp.exp(-jnp.abs(x)))


def _params(sem):
    return pltpu.CompilerParams(dimension_semantics=sem, vmem_limit_bytes=VMEM_LIMIT)


def _const_spec(shape):
    nd = len(shape)
    return pl.BlockSpec(shape, lambda *_: (0,) * nd, pipeline_mode=pl.Buffered(1))


def _row_tile(n, pref):
    return pref if n % pref == 0 else n


def _in_proj_kernel(x_ref, g_ref, wm_ref, wf_ref, bf_ref,
                    u_ref, q_ref, k_ref, v_ref, kb_ref, vb_ref, pc_ref, lf_ref):
    h = _rms(x_ref[...], g_ref[...]).astype(BF16)
    u_ref[...] = _dot(h, wm_ref[:, 0:WIDTH])
    q_ref[...] = (_dot(h, wm_ref[:, WIDTH:2 * WIDTH]) * ATT_SCALE).astype(q_ref.dtype)
    k = _dot(h, wm_ref[:, 2 * WIDTH:3 * WIDTH])
    k_ref[...] = k
    kb_ref[...] = k.astype(BF16)
    v = _dot(h, wm_ref[:, 3 * WIDTH:4 * WIDTH])
    v_ref[...] = v
    vb_ref[...] = v.astype(BF16)
    pc_ref[...] = _dot(h, wm_ref[:, 4 * WIDTH:4 * WIDTH + C_IN])
    f = _dot(h, wf_ref[...]) + bf_ref[...]
    lf_ref[...] = -_softplus(-f)


IN_TT = 64


def _in_proj_prompt_kernel(x_ref, g_ref, wm_ref, wf_ref, bf_ref, perm_ref,
                           u_ref, q_ref, k_ref, v_ref, kb_ref, vb_ref, pc_ref, lf_ref):
    nb, tt, _ = x_ref.shape
    rows = nb * tt
    h = _rms(x_ref[...].reshape(rows, D_MODEL), g_ref[...]).astype(BF16)
    put = lambda ref, val: ref.__setitem__(Ellipsis, val.reshape(nb, tt, val.shape[-1]))
    put(u_ref, _dot(h, wm_ref[:, 0:WIDTH]))
    put(q_ref, (_dot(h, wm_ref[:, WIDTH:2 * WIDTH]) * ATT_SCALE).astype(q_ref.dtype))
    k = _dot(h, wm_ref[:, 2 * WIDTH:3 * WIDTH])
    put(k_ref, k)
    put(kb_ref, k.astype(BF16))
    v = _dot(h, wm_ref[:, 3 * WIDTH:4 * WIDTH])
    put(v_ref, v)
    put(vb_ref, v.astype(BF16))
    put(lf_ref, -_softplus(-(_dot(h, wf_ref[...]) + bf_ref[...])))
    h_tm = _dot(perm_ref[...], h).astype(BF16)
    pc_ref[...] = _dot(h_tm, wm_ref[:, 4 * WIDTH:4 * WIDTH + C_IN]).reshape(tt, nb, C_IN)


def _in_proj_prompt(x, g, wm, wf, bf):
    nb, t, _ = x.shape
    tt = _row_tile(t, IN_TT)
    r = jnp.arange(nb * tt)
    perm = ((r[:, None] % nb) * tt + r[:, None] // nb == r[None, :]).astype(BF16)
    bm = lambda w: pl.BlockSpec((nb, tt, w), lambda i: (0, i, 0))
    sds = lambda w, dt: jax.ShapeDtypeStruct((nb, t, w), dt)
    return pl.pallas_call(
        _in_proj_prompt_kernel,
        grid=(t // tt,),
        in_specs=[bm(D_MODEL), _const_spec((1, D_MODEL)), _const_spec(wm.shape),
                  _const_spec(wf.shape), _const_spec((1, LANES)), _const_spec(perm.shape)],
        out_specs=(bm(WIDTH), bm(WIDTH), bm(WIDTH), bm(WIDTH), bm(WIDTH), bm(WIDTH),
                   pl.BlockSpec((tt, nb, C_IN), lambda i: (i, 0, 0)), bm(LANES)),
        out_shape=(sds(WIDTH, F32), sds(WIDTH, BF16), sds(WIDTH, F32), sds(WIDTH, F32),
                   sds(WIDTH, BF16), sds(WIDTH, BF16),
                   jax.ShapeDtypeStruct((t, nb, C_IN), F32), sds(LANES, F32)),
        compiler_params=_params(("parallel",)),
        name="in_proj_prompt",
    )(x, g, wm, wf, bf, perm)


def _in_proj(x, g, wm, wf, bf, q_dtype):
    n = x.shape[0]
    tm = _row_tile(n, 512)
    row = lambda w: pl.BlockSpec((tm, w), lambda i: (i, 0))
    out_shape = (
        jax.ShapeDtypeStruct((n, WIDTH), F32),
        jax.ShapeDtypeStruct((n, WIDTH), q_dtype),
        jax.ShapeDtypeStruct((n, WIDTH), F32),
        jax.ShapeDtypeStruct((n, WIDTH), F32),
        jax.ShapeDtypeStruct((n, WIDTH), BF16),
        jax.ShapeDtypeStruct((n, WIDTH), BF16),
        jax.ShapeDtypeStruct((n, C_IN), F32),
        jax.ShapeDtypeStruct((n, LANES), F32),
    )
    return pl.pallas_call(
        _in_proj_kernel,
        grid=(n // tm,),
        in_specs=[row(D_MODEL), _const_spec((1, D_MODEL)), _const_spec(wm.shape),
                  _const_spec(wf.shape), _const_spec((1, LANES))],
        out_specs=(row(WIDTH), row(WIDTH), row(WIDTH), row(WIDTH), row(WIDTH), row(WIDTH),
                   row(C_IN), row(LANES)),
        out_shape=out_shape,
        compiler_params=_params(("parallel",)),
        name="in_proj",
    )(x, g, wm, wf, bf)


CUM_BLK = 256


BIAS_TERMS = 3


def _cumsum_kernel(lf_ref, tri_ref, pq_ref, pk_ref, oq_ref, ok_ref, eq_ref, ek_ref):
    t = lf_ref.shape[1]
    carry = jnp.zeros((1, LANES), F32)
    for i in range(t // CUM_BLK):
        sl = slice(i * CUM_BLK, (i + 1) * CUM_BLK)
        c = _dot_right3(tri_ref[...], lf_ref[0, sl, :]) + carry
        terms = jnp.concatenate(_split3(c), axis=1)
        eq_ref[0, sl, :] = (_dot(terms, pq_ref[...]) + oq_ref[...]).astype(BF16)
        ek_ref[0, sl, :] = (ok_ref[...] - _dot(terms, pk_ref[...])).astype(BF16)
        carry = c[CUM_BLK - 1:CUM_BLK, :]


def _bias_lane_mats():
    src = jnp.arange(BIAS_TERMS * LANES)
    part, head = src // LANES, src % LANES
    dst = jnp.arange(NH * LANES)
    dhead, dlane = dst // LANES, dst % LANES
    other = HD * (1 - dhead % 2)
    slot = dlane - other
    same = head[:, None] == dhead[None, :]
    pq = (same & (slot[None, :] == part[:, None])).astype(BF16)
    pk = (same & (slot[None, :] == BIAS_TERMS + part[:, None])).astype(BF16)
    oq = ((slot >= BIAS_TERMS) & (slot < 2 * BIAS_TERMS)).astype(F32).reshape(1, -1)
    ok = ((slot >= 0) & (slot < BIAS_TERMS)).astype(F32).reshape(1, -1)
    return pq, pk, oq, ok


def _cumsum(lf, tri, mats):
    b, t, _ = lf.shape
    out = pl.BlockSpec((1, t, NH * LANES), lambda i: (i, 0, 0))
    return pl.pallas_call(
        _cumsum_kernel,
        grid=(b,),
        in_specs=[pl.BlockSpec((1, t, LANES), lambda i: (i, 0, 0)), _const_spec(tri.shape)]
                 + [_const_spec(m.shape) for m in mats],
        out_specs=(out, out),
        out_shape=(jax.ShapeDtypeStruct((b, t, NH * LANES), BF16),
                   jax.ShapeDtypeStruct((b, t, NH * LANES), BF16)),
        compiler_params=_params(("parallel",)),
        name="fox_cumsum",
    )(lf, tri, *mats)


def _flash_kernel(q_ref, k_ref, v_ref, eq_ref, ek_ref, o_ref, m_scr, l_scr, acc_scr, *, tq, tk):
    qi = pl.program_id(1)
    ki = pl.program_id(2)

    @pl.when(ki == 0)
    def _():
        m_scr[...] = jnp.full(m_scr.shape, NEG_INF, F32)
        l_scr[...] = jnp.zeros(l_scr.shape, F32)
        acc_scr[...] = jnp.zeros(acc_scr.shape, F32)

    def block(on_diagonal):
        lane = lax.broadcasted_iota(jnp.int32, (1, LANES), 1)
        own = [((lane // HD) == j).astype(F32).astype(BF16) for j in range(2)]
        if on_diagonal:
            causal = (lax.broadcasted_iota(jnp.int32, (tq, tk), 1)
                      <= lax.broadcasted_iota(jnp.int32, (tq, tk), 0))
        for hp in range(NHP):
            ls = slice(hp * LANES, (hp + 1) * LANES)
            q = q_ref[0, :, ls]
            k = k_ref[0, :, ls]
            v = v_ref[0, :, ls]
            pv = []
            alpha = []
            for j in range(2):
                h = 2 * hp + j
                hs = slice(h * LANES, (h + 1) * LANES)
                s = _dot_nt(q * own[j] + eq_ref[0, :, hs], k * own[j] + ek_ref[0, :, hs])
                if on_diagonal:
                    s = jnp.where(causal, s, NEG_INF)
                m_prev = m_scr[h]
                m_new = jnp.maximum(m_prev, jnp.max(s, axis=1, keepdims=True))
                a = jnp.exp(m_prev - m_new)
                p = jnp.exp(s - m_new[:, 0:1])
                l_scr[h] = a * l_scr[h] + jnp.sum(p, axis=1, keepdims=True)
                m_scr[h] = m_new
                pv.append(_dot(p.astype(BF16), v))
                alpha.append(a)
            first = (lane // HD) == 0
            acc_scr[hp] = (jnp.where(first, alpha[0], alpha[1]) * acc_scr[hp]
                           + jnp.where(first, pv[0], pv[1]))

    pl.when(ki < qi)(functools.partial(block, False))
    pl.when(ki == qi)(functools.partial(block, True))

    @pl.when(ki == qi)
    def _():
        lane = lax.broadcasted_iota(jnp.int32, (1, LANES), 1)
        for hp in range(NHP):
            l = jnp.where((lane // HD) == 0, l_scr[2 * hp], l_scr[2 * hp + 1])
            o_ref[0, :, hp * LANES:(hp + 1) * LANES] = (acc_scr[hp] / l).astype(o_ref.dtype)


def _flash(q, k, v, eq, ek, *, tq):
    b, t, _ = q.shape
    tq = min(tq, t)
    nq = t // tq
    kv_idx = lambda bi, qi, ki: (bi, jnp.minimum(ki, qi), 0)
    return pl.pallas_call(
        functools.partial(_flash_kernel, tq=tq, tk=tq),
        grid=(b, nq, nq),
        in_specs=[pl.BlockSpec((1, tq, WIDTH), lambda bi, qi, ki: (bi, qi, 0)),
                  pl.BlockSpec((1, tq, WIDTH), kv_idx),
                  pl.BlockSpec((1, tq, WIDTH), kv_idx),
                  pl.BlockSpec((1, tq, NH * LANES), lambda bi, qi, ki: (bi, qi, 0)),
                  pl.BlockSpec((1, tq, NH * LANES), kv_idx)],
        out_specs=pl.BlockSpec((1, tq, WIDTH), lambda bi, qi, ki: (bi, qi, 0)),
        out_shape=jax.ShapeDtypeStruct((b, t, WIDTH), BF16),
        scratch_shapes=[pltpu.VMEM((NH, tq, LANES), F32), pltpu.VMEM((NH, tq, LANES), F32),
                        pltpu.VMEM((NHP, tq, LANES), F32)],
        compiler_params=_params(("parallel", "parallel", "arbitrary")),
        name="fox_flash",
    )(q, k, v, eq, ek)


def _decode_bias_kernel(pt_ref, tbl_ref, lfn_ref, msuf_ref, mtot_ref, plater_ref, o_ref, g_scr):
    b = pl.program_id(0)
    npages = g_scr.shape[0]
    for p in range(npages):
        g_scr[p] = tbl_ref[pt_ref[b, p]]
    g = g_scr[...].reshape(npages * NH, PAGE_SIZE)
    within = _dot_left3(g, msuf_ref[...])
    tot = _dot_left3(g, mtot_ref[...])
    later = _dot_right3(plater_ref[...], tot)
    o_ref[0] = (within + later).reshape(npages, NH, PAGE_SIZE) + lfn_ref[...]


def _decode_bias(page_table, tbl, lfn, msuf, mtot, plater):
    nb, npages = page_table.shape
    grid_spec = pltpu.PrefetchScalarGridSpec(
        num_scalar_prefetch=1,
        grid=(nb,),
        in_specs=[_const_spec(tbl.shape),
                  pl.BlockSpec((1, NH, PAGE_SIZE), lambda b, pt: (b, 0, 0)),
                  _const_spec(msuf.shape), _const_spec(mtot.shape), _const_spec(plater.shape)],
        out_specs=pl.BlockSpec((1, npages, NH, PAGE_SIZE), lambda b, pt: (b, 0, 0, 0)),
        scratch_shapes=[pltpu.VMEM((npages, NH, PAGE_SIZE), F32)],
    )
    return pl.pallas_call(
        _decode_bias_kernel,
        grid_spec=grid_spec,
        out_shape=jax.ShapeDtypeStruct((nb, npages, NH, PAGE_SIZE), F32),
        compiler_params=_params(("arbitrary",)),
        name="fox_decode_bias",
    )(page_table, tbl, lfn, msuf, mtot, plater)


PAGES_PER_STEP = 16


def _col_bcast(row_pair, eye2, j):
    lane = lax.broadcasted_iota(jnp.int32, (1, LANES), 1)
    diag = jnp.where((lane // HD) == j, jnp.broadcast_to(row_pair, (HD, LANES)) * eye2, 0.0)
    return jnp.broadcast_to(jnp.sum(diag, axis=1, keepdims=True), (HD, LANES))


def _decode_attn_kernel(pt_ref, q_ref, kn_ref, vn_ref, bias_ref, eye_ref, *refs):
    k_refs = refs[0:PAGES_PER_STEP]
    v_refs = refs[PAGES_PER_STEP:2 * PAGES_PER_STEP]
    o_ref, m_scr, l_scr, acc_scr, qc_scr = refs[2 * PAGES_PER_STEP:]
    g = pl.program_id(1)
    eye2 = eye_ref[...]
    lane = lax.broadcasted_iota(jnp.int32, (1, LANES), 1)
    sub = lax.broadcasted_iota(jnp.int32, (NH, PAGE_SIZE), 0)

    @pl.when(g == 0)
    def _():
        q = q_ref[0]
        own = (lax.broadcasted_iota(jnp.int32, (NH, WIDTH), 1) // HD
               == lax.broadcasted_iota(jnp.int32, (NH, WIDTH), 0))
        qk = jnp.broadcast_to(q * kn_ref[0], (NH, WIDTH))
        s_new = jnp.sum(jnp.where(own, qk, 0.0), axis=1, keepdims=True)
        m_scr[...] = jnp.broadcast_to(s_new, m_scr.shape)
        l_scr[...] = jnp.ones(l_scr.shape, F32)
        vn = vn_ref[0]
        for h in range(NH):
            ls = slice((h // 2) * LANES, (h // 2 + 1) * LANES)
            qc_scr[h] = _col_bcast(q[:, ls], eye2, h % 2)
            acc_scr[h] = jnp.where(lane == 0, _col_bcast(vn[:, ls], eye2, h % 2), 0.0)

    s_parts = []
    for j in range(PAGES_PER_STEP):
        s_page = jnp.zeros((NH, PAGE_SIZE), F32)
        for h in range(NH):
            sh = jnp.sum(k_refs[j][0, 0, h] * qc_scr[h], axis=0, keepdims=True)
            s_page = jnp.where(sub == h, jnp.broadcast_to(sh, (NH, PAGE_SIZE)), s_page)
        s_parts.append(s_page + bias_ref[0, j])
    s = jnp.concatenate(s_parts, axis=1)
    m_prev = m_scr[...]
    m_new = jnp.maximum(m_prev, jnp.max(s, axis=1, keepdims=True))
    a = jnp.exp(m_prev - m_new)
    p = jnp.exp(s - m_new[:, 0:1])
    l_scr[...] = a * l_scr[...] + jnp.sum(p, axis=1, keepdims=True)
    m_scr[...] = m_new
    for h in range(NH):
        acc = acc_scr[h] * a[h:h + 1, 0:1]
        for j in range(PAGES_PER_STEP):
            ph = p[h:h + 1, j * PAGE_SIZE:(j + 1) * PAGE_SIZE]
            acc = acc + v_refs[j][0, 0, h] * jnp.broadcast_to(ph, (HD, PAGE_SIZE))
        acc_scr[h] = acc

    @pl.when(g == pl.num_programs(1) - 1)
    def _():
        rows = []
        for hp in range(NHP):
            pair = []
            for j in range(2):
                h = 2 * hp + j
                col = jnp.sum(acc_scr[h], axis=1, keepdims=True) / l_scr[h:h + 1, 0:1]
                pair.append(jnp.sum(jnp.broadcast_to(col, (HD, LANES)) * eye2, axis=0,
                                    keepdims=True))
            rows.append(jnp.where((lane // HD) == 0, pair[0], pair[1]))
        o_ref[0] = jnp.concatenate(rows, axis=1)


def _decode_attn(page_table, q, k_new, v_new, bias, eye2, cache_kt, cache_vt, layer):
    nb, npages = page_table.shape
    ng = npages // PAGES_PER_STEP

    def page_spec(j):
        return pl.BlockSpec((1, 1, NH, HD, PAGE_SIZE),
                            lambda b, g, pt: (layer, pt[b, g * PAGES_PER_STEP + j], 0, 0, 0))

    vec = lambda: pl.BlockSpec((1, 1, WIDTH), lambda b, g, pt: (b, 0, 0))
    grid_spec = pltpu.PrefetchScalarGridSpec(
        num_scalar_prefetch=1,
        grid=(nb, ng),
        in_specs=[vec(), vec(), vec(),
                  pl.BlockSpec((1, PAGES_PER_STEP, NH, PAGE_SIZE), lambda b, g, pt: (b, g, 0, 0)),
                  _const_spec(eye2.shape)]
                 + [page_spec(j) for j in range(PAGES_PER_STEP)]
                 + [page_spec(j) for j in range(PAGES_PER_STEP)],
        out_specs=pl.BlockSpec((1, 1, WIDTH), lambda b, g, pt: (b, 0, 0)),
        scratch_shapes=[pltpu.VMEM((NH, LANES), F32), pltpu.VMEM((NH, LANES), F32),
                        pltpu.VMEM((NH, HD, PAGE_SIZE), F32), pltpu.VMEM((NH, HD, LANES), F32)],
    )
    return pl.pallas_call(
        _decode_attn_kernel,
        grid_spec=grid_spec,
        out_shape=jax.ShapeDtypeStruct((nb, 1, WIDTH), F32),
        compiler_params=_params(("parallel", "arbitrary")),
        name="fox_decode_attn",
    )(page_table, q, k_new, v_new, bias, eye2,
      *([cache_kt] * PAGES_PER_STEP), *([cache_vt] * PAGES_PER_STEP))


POOL_HALO = 16


def _pool_project(pooled, wp_ref, sc_ref):
    outs = []
    for g in range(len(POOL_WINDOWS)):
        ls = slice(g * GROUP_W, (g + 1) * GROUP_W)
        outs.append(_dot(pooled[g].astype(BF16), wp_ref[g]) * sc_ref[:, ls])
    return outs


def _pool_prompt_kernel(u_ref, wp_ref, sc_ref, o_ref, ext_scr, *, tt):
    ti = pl.program_id(1)

    @pl.when(ti == 0)
    def _():
        ext_scr[0:POOL_HALO, :] = jnp.zeros((POOL_HALO, A_WIDTH), F32)

    ext_scr[POOL_HALO:POOL_HALO + tt, :] = u_ref[0]
    pos1 = ti * tt + lax.broadcasted_iota(jnp.int32, (tt, GROUP_W), 0) + 1
    pooled = []
    for g, w in enumerate(POOL_WINDOWS):
        ls = slice(g * GROUP_W, (g + 1) * GROUP_W)
        x = ext_scr[POOL_HALO:POOL_HALO + tt, ls]
        win = x
        for d in range(1, w):
            win = win + ext_scr[POOL_HALO - d:POOL_HALO - d + tt, ls]
        cnt = jnp.minimum(w, pos1).astype(F32)
        pooled.append(win / cnt - x)
    outs = _pool_project(pooled, wp_ref, sc_ref)
    for g in range(len(POOL_WINDOWS)):
        o_ref[0, :, g * GROUP_W:(g + 1) * GROUP_W] = outs[g].astype(o_ref.dtype)
    ext_scr[0:POOL_HALO, :] = ext_scr[tt:tt + POOL_HALO, :]


def _pool_prompt(u, wp, sc):
    b, t, _ = u.shape
    tt = _row_tile(t, 512)
    return pl.pallas_call(
        functools.partial(_pool_prompt_kernel, tt=tt),
        grid=(b, t // tt),
        in_specs=[pl.BlockSpec((1, tt, A_WIDTH), lambda bi, ti: (bi, ti, 0)),
                  _const_spec(wp.shape), _const_spec((1, A_WIDTH))],
        out_specs=pl.BlockSpec((1, tt, A_WIDTH), lambda bi, ti: (bi, ti, 0)),
        out_shape=jax.ShapeDtypeStruct((b, t, A_WIDTH), BF16),
        scratch_shapes=[pltpu.VMEM((POOL_HALO + tt, A_WIDTH), F32)],
        compiler_params=_params(("parallel", "arbitrary")),
        name="pool_prompt",
    )(u, wp, sc)


def _pool_sample_kernel(u_ref, prev_ref, wp_ref, sc_ref, o_ref):
    pooled = []
    for g, w in enumerate(POOL_WINDOWS):
        ls = slice(g * GROUP_W, (g + 1) * GROUP_W)
        x = u_ref[:, ls]
        win = x
        for d in range(1, w):
            win = win + prev_ref[POOL_PREFIX - d, :, ls]
        pooled.append(win / float(w) - x)
    outs = _pool_project(pooled, wp_ref, sc_ref)
    for g in range(len(POOL_WINDOWS)):
        o_ref[:, g * GROUP_W:(g + 1) * GROUP_W] = outs[g]


def _pool_sample(u, prev_t, wp, sc):
    nb = u.shape[0]
    return pl.pallas_call(
        _pool_sample_kernel,
        out_shape=jax.ShapeDtypeStruct((nb, A_WIDTH), F32),
        compiler_params=pltpu.CompilerParams(vmem_limit_bytes=VMEM_LIMIT),
        name="pool_sample",
    )(u, prev_t, wp, sc)


SCAN_BB = 8


def _rwkv_prep_math(pc, prev, mu_ref, wlo_ref, w0_ref, a0_ref, gup_ref, kk_ref, ka_ref, rk_ref,
                    seg_ref):
    xs = pc + (prev - pc) * mu_ref[...]
    r = xs[:, 0:WIDTH]
    k = xs[:, WIDTH:2 * WIDTH]
    v = xs[:, 2 * WIDTH:3 * WIDTH]
    lora = xs[:, 3 * WIDTH:3 * WIDTH + LANES]
    gd = xs[:, 3 * WIDTH + LANES:3 * WIDTH + 2 * LANES]
    lane = lax.broadcasted_iota(jnp.int32, (1, LANES), 1)
    lhs = jnp.where(lane < DECAY_LORA, jnp.tanh(lora), lora).astype(BF16)
    up = _dot(lhs, wlo_ref[...])
    w = -_softplus(-(w0_ref[...] + up[:, 0:WIDTH])) - 0.5
    decay = jnp.exp(-jnp.exp(w))
    a = _sigmoid(a0_ref[...] + up[:, WIDTH:2 * WIDTH])
    g = _dot(_sigmoid(gd).astype(BF16), gup_ref[...])
    kk = k * kk_ref[...]
    nrm = jnp.sqrt(_dot_left2(kk * kk, seg_ref[...]))
    kk = kk / jnp.maximum(nrm, 1e-12)
    k2 = k * (1.0 + (a - 1.0) * ka_ref[...])
    bonus = _dot_left2(r * k2 * rk_ref[...], seg_ref[...]) * v
    return r, decay, k2, v, -kk, kk * a, g, bonus


def _rwkv_prep_prompt_kernel(pc_ref, mu_ref, wlo_ref, w0_ref, a0_ref, gup_ref, kk_ref, ka_ref,
                             rk_ref, seg_ref, *refs, tt):
    out_refs, last_scr = refs[:-1], refs[-1]
    ti = pl.program_id(0)

    @pl.when(ti == 0)
    def _():
        last_scr[...] = jnp.zeros(last_scr.shape, F32)

    pc3 = pc_ref[...]
    prev3 = jnp.concatenate([last_scr[...], pc3[0:tt - 1]], axis=0)
    last_scr[...] = pc3[tt - 1:tt]
    rows = tt * SCAN_BB
    outs = _rwkv_prep_math(pc3.reshape(rows, C_IN), prev3.reshape(rows, C_IN), mu_ref, wlo_ref,
                           w0_ref, a0_ref, gup_ref, kk_ref, ka_ref, rk_ref, seg_ref)
    for o_ref, val in zip(out_refs, outs):
        o_ref[...] = val.reshape(tt, SCAN_BB, WIDTH)


def _rwkv_prep_prompt(pc_tm, weights):
    t, b, _ = pc_tm.shape
    assert b == SCAN_BB
    tt = _row_tile(t, 64)
    tile = lambda w: pl.BlockSpec((tt, b, w), lambda ti: (ti, 0, 0))
    return pl.pallas_call(
        functools.partial(_rwkv_prep_prompt_kernel, tt=tt),
        grid=(t // tt,),
        in_specs=[tile(C_IN)] + [_const_spec(w.shape) for w in weights],
        out_specs=tuple(tile(WIDTH) for _ in range(8)),
        out_shape=tuple(jax.ShapeDtypeStruct((t, b, WIDTH), F32) for _ in range(8)),
        scratch_shapes=[pltpu.VMEM((1, b, C_IN), F32)],
        compiler_params=_params(("arbitrary",)),
        name="rwkv_prep_prompt",
    )(pc_tm, *weights)


def _rwkv_prep_sample_kernel(pc_ref, prev_ref, mu_ref, wlo_ref, w0_ref, a0_ref, gup_ref, kk_ref,
                             ka_ref, rk_ref, seg_ref, *out_refs):
    outs = _rwkv_prep_math(pc_ref[...], prev_ref[...], mu_ref, wlo_ref, w0_ref, a0_ref, gup_ref,
                           kk_ref, ka_ref, rk_ref, seg_ref)
    for o_ref, val in zip(out_refs, outs):
        o_ref[...] = val


def _rwkv_prep_sample(pc, prev, weights):
    nb = pc.shape[0]
    return pl.pallas_call(
        _rwkv_prep_sample_kernel,
        out_shape=tuple(jax.ShapeDtypeStruct((nb, WIDTH), F32) for _ in range(8)),
        compiler_params=pltpu.CompilerParams(vmem_limit_bytes=VMEM_LIMIT),
        name="rwkv_prep_sample",
    )(pc, prev, *weights)


def _scan_kernel(r_ref, w_ref, k_ref, v_ref, kn_ref, b_ref, s0_ref, seg_ref, eye_ref,
                 y_ref, so_ref, s_scr, *, tc, zero_init):
    ci = pl.program_id(1)

    @pl.when(ci == 0)
    def _():
        if zero_init:
            s_scr[...] = jnp.zeros(s_scr.shape, F32)
        else:
            s_scr[...] = s0_ref[...]

    seg = seg_ref[...]
    eye2 = eye_ref[...]
    groups = [(b, hp) for b in range(SCAN_BB) for hp in range(NHP)]
    ng = len(groups)
    sub = lax.broadcasted_iota(jnp.int32, (SCAN_BB, LANES), 0)

    def row(slab, b, hp):
        return jnp.broadcast_to(slab[b:b + 1, hp * LANES:(hp + 1) * LANES], (HD, LANES))

    def row16(slab, b, hp):
        return row(slab, b, hp).astype(BF16)

    def seg_sum(parts):
        return _dot(jnp.concatenate(parts, axis=0), seg)

    eye16 = eye2.astype(BF16)

    def write_y(yb, t):
        y_tiles = [jnp.zeros((SCAN_BB, LANES), F32) for _ in range(NHP)]
        for n, (b, hp) in enumerate(groups):
            y_row = jnp.sum(yb[n * HD:(n + 1) * HD] * eye2, axis=0, keepdims=True)
            y_tiles[hp] = jnp.where(sub == b, jnp.broadcast_to(y_row, (SCAN_BB, LANES)),
                                    y_tiles[hp])
        y_ref[t] = jnp.concatenate(y_tiles, axis=1)

    def step(t, carry):
        tp = jnp.maximum(t - 1, 0)
        w_t, k_t, v_t, kn_t, b_t = (ref[t] for ref in (w_ref, k_ref, v_ref, kn_ref, b_ref))
        r_p = r_ref[tp]
        s16 = [s_scr[b, hp].astype(BF16) for b, hp in groups]
        res = seg_sum([s16[n] * row16(kn_t, b, hp) for n, (b, hp) in enumerate(groups)]
                      + [eye16 * row16(v_t, b, hp) for b, hp in groups]
                      + [s16[n] * row16(r_p, b, hp) for n, (b, hp) in enumerate(groups)])
        for n, (b, hp) in enumerate(groups):
            sa = res[n * HD:(n + 1) * HD]
            vb = res[(ng + n) * HD:(ng + n + 1) * HD]
            s_scr[b, hp] = (s_scr[b, hp] * row(w_t, b, hp) + sa * row(b_t, b, hp)
                            + vb * row(k_t, b, hp))
        write_y(res[2 * ng * HD:], tp)
        return carry

    lax.fori_loop(0, tc, step, 0)
    r_last = r_ref[tc - 1]
    write_y(seg_sum([s_scr[b, hp].astype(BF16) * row16(r_last, b, hp) for b, hp in groups]),
            tc - 1)

    @pl.when(ci == pl.num_programs(1) - 1)
    def _():
        so_ref[...] = s_scr[...]


def _scan(r, w, k, v, kn, bv, s0, seg2, eye2, *, tc):
    t, nb, _ = r.shape
    tc = min(tc, t)
    zero_init = s0 is None
    if zero_init:
        s0 = jnp.zeros((SCAN_BB, NHP, HD, LANES), F32)
        s0_spec = pl.BlockSpec((SCAN_BB, NHP, HD, LANES), lambda bi, ci: (0, 0, 0, 0))
    else:
        s0_spec = pl.BlockSpec((SCAN_BB, NHP, HD, LANES), lambda bi, ci: (bi, 0, 0, 0))
    seq = lambda: pl.BlockSpec((tc, SCAN_BB, WIDTH), lambda bi, ci: (ci, bi, 0))
    return pl.pallas_call(
        functools.partial(_scan_kernel, tc=tc, zero_init=zero_init),
        grid=(nb // SCAN_BB, t // tc),
        in_specs=[seq(), seq(), seq(), seq(), seq(), seq(), s0_spec,
                  _const_spec(seg2.shape), _const_spec(eye2.shape)],
        out_specs=(seq(), pl.BlockSpec((SCAN_BB, NHP, HD, LANES), lambda bi, ci: (bi, 0, 0, 0))),
        out_shape=(jax.ShapeDtypeStruct((t, nb, WIDTH), F32),
                   jax.ShapeDtypeStruct((nb, NHP, HD, LANES), F32)),
        scratch_shapes=[pltpu.VMEM((SCAN_BB, NHP, HD, LANES), F32)],
        compiler_params=_params(("parallel", "arbitrary")),
        name="rwkv_scan",
    )(r, w, k, v, kn, bv, s0, seg2, eye2)


def _rwkv_post_kernel(y_ref, bonus_ref, g_ref, lnw_ref, lnb_ref, seg_ref, o_ref):
    y = y_ref[...]
    seg = seg_ref[...]
    mean = _dot_left2(y, seg) * (1.0 / HD)
    d = y - mean
    var = _dot_left2(d * d, seg) * (1.0 / HD)
    yn = d * lax.rsqrt(var + LN_X_EPS) * lnw_ref[...] + lnb_ref[...]
    o_ref[...] = ((yn + bonus_ref[...]) * g_ref[...]).astype(o_ref.dtype)


def _rwkv_post(y, bonus, g, lnw, lnb, seg, out_dtype):
    n = y.shape[0]
    tm = _row_tile(n, 512)
    row = pl.BlockSpec((tm, WIDTH), lambda i: (i, 0))
    return pl.pallas_call(
        _rwkv_post_kernel,
        grid=(n // tm,),
        in_specs=[row, row, row, _const_spec((1, WIDTH)), _const_spec((1, WIDTH)),
                  _const_spec(seg.shape)],
        out_specs=row,
        out_shape=jax.ShapeDtypeStruct((n, WIDTH), out_dtype),
        compiler_params=_params(("parallel",)),
        name="rwkv_post",
    )(y, bonus, g, lnw, lnb, seg)


def _merge_kernel(x_ref, ya_ref, yb_ref, yc_ref, gpre_ref, gpost_ref, wg_ref, wa_ref, wb_ref,
                  wc_ref, wo_ref, o_ref):
    x = x_ref[...]
    h = _rms(x, gpre_ref[...]).astype(BF16)
    merged = None
    for n, (y_ref, w_ref) in enumerate(((ya_ref, wa_ref), (yb_ref, wb_ref), (yc_ref, wc_ref))):
        gate = _sigmoid(_dot(h, wg_ref[:, n * D_MODEL:(n + 1) * D_MODEL]))
        term = gate * _dot(y_ref[...].astype(BF16), w_ref[...])
        merged = term if merged is None else merged + term
    o = _dot(merged.astype(BF16), wo_ref[...])
    o_ref[...] = x + _rms(o, gpost_ref[...])


def _merge(x, ya, yb, yc, gpre, gpost, wg, wa, wb, wc, wo):
    n = x.shape[0]
    tm = _row_tile(n, 512)
    row = lambda w: pl.BlockSpec((tm, w), lambda i: (i, 0))
    return pl.pallas_call(
        _merge_kernel,
        grid=(n // tm,),
        in_specs=[row(D_MODEL), row(WIDTH), row(WIDTH), row(WIDTH),
                  _const_spec((1, D_MODEL)), _const_spec((1, D_MODEL)),
                  _const_spec(wg.shape), _const_spec(wa.shape), _const_spec(wb.shape),
                  _const_spec(wc.shape), _const_spec(wo.shape)],
        out_specs=row(D_MODEL),
        out_shape=jax.ShapeDtypeStruct((n, D_MODEL), F32),
        compiler_params=_params(("parallel",)),
        name="merge",
    )(x, ya, yb, yc, gpre, gpost, wg, wa, wb, wc, wo)


def _mlp_kernel(x_ref, gpre_ref, gpost_ref, wu_ref, wd_ref, o_ref):
    x = x_ref[...]
    h = _rms(x, gpre_ref[...]).astype(BF16)
    up = jnp.maximum(_dot(h, wu_ref[...]), 0.0)
    down = _dot((up * up).astype(BF16), wd_ref[...])
    o_ref[...] = x + _rms(down, gpost_ref[...])


def _mlp(x, gpre, gpost, wu, wd):
    n = x.shape[0]
    tm = _row_tile(n, 512)
    row = pl.BlockSpec((tm, D_MODEL), lambda i: (i, 0))
    return pl.pallas_call(
        _mlp_kernel,
        grid=(n // tm,),
        in_specs=[row, _const_spec((1, D_MODEL)), _const_spec((1, D_MODEL)),
                  _const_spec(wu.shape), _const_spec(wd.shape)],
        out_specs=row,
        out_shape=jax.ShapeDtypeStruct((n, D_MODEL), F32),
        compiler_params=_params(("parallel",)),
        name="mlp",
    )(x, gpre, gpost, wu, wd)


def _seg_ones(n):
    i = jnp.arange(n)
    return (i[:, None] // HD == i[None, :] // HD).astype(BF16)


def _decode_bias_mats(npages):
    pos = jnp.arange(PAGE_SIZE)
    msuf = (pos[:, None] > pos[None, :]).astype(BF16)
    mtot = jnp.ones((PAGE_SIZE, PAGE_SIZE), BF16)
    r = jnp.arange(npages * NH)
    plater = ((r[None, :] // NH > r[:, None] // NH)
              & (r[None, :] % NH == r[:, None] % NH)).astype(BF16)
    return msuf, mtot, plater


def _state_to_tiles(s):
    nb = s.shape[0]
    return s.reshape(nb, NHP, 2, HD, HD).transpose(0, 1, 3, 2, 4).reshape(nb, NHP, HD, LANES)


def _tiles_to_state(s):
    nb = s.shape[0]
    return s.reshape(nb, NHP, HD, 2, HD).transpose(0, 1, 3, 2, 4).reshape(nb, NH, HD, HD)


def _layer_weights(l, p):
    w_in = p["w_in"][l]
    row = lambda a: a[l].reshape(1, -1)
    wlo = jnp.zeros((LANES, 2 * WIDTH), F32)
    wlo = wlo.at[0:DECAY_LORA, 0:WIDTH].set(p["rwkv_w_up"][l])
    wlo = wlo.at[DECAY_LORA:, WIDTH:].set(p["rwkv_a_up"][l])
    return dict(
        g_mix_pre=row(p["norm_mix_pre"]), g_mix_post=row(p["norm_mix_post"]),
        g_mlp_pre=row(p["norm_mlp_pre"]), g_mlp_post=row(p["norm_mlp_post"]),
        wm=jnp.concatenate([w_in[:, 0:OFF_F], w_in[:, OFF_C:OFF_G]], axis=1).astype(BF16),
        wf=jnp.pad(w_in[:, OFF_F:OFF_C], ((0, 0), (0, LANES - NH))).astype(BF16),
        bf=jnp.pad(p["b_forget"][l], (0, LANES - NH)).reshape(1, LANES),
        wg=w_in[:, OFF_G:].astype(BF16),
        wp=p["w_pool"][l].astype(BF16), pool_scale=row(p["pool_scale"]),
        rw=(row(p["rwkv_mu"]), wlo.astype(BF16), row(p["rwkv_w0"]), row(p["rwkv_a0"]),
            p["rwkv_g_up"][l].astype(BF16), row(p["rwkv_k_k"]), row(p["rwkv_k_a"]),
            row(p["rwkv_r_k"])),
        lnw=row(p["rwkv_ln_w"]), lnb=row(p["rwkv_ln_b"]),
        wa=p["w_branch_a"][l].astype(BF16), wb=p["w_branch_b"][l].astype(BF16),
        wc=p["w_branch_c"][l].astype(BF16), wo=p["w_out"][l].astype(BF16),
        wu=p["w_mlp_up"][l].astype(BF16), wd=p["w_mlp_down"][l].astype(BF16),
    )


def _prompt_layer(x, lw, consts):
    b, t, _ = x.shape
    n = b * t
    x2 = x.reshape(n, D_MODEL)
    u, q, k, v, kb, vb, pc_tm, lf = _in_proj_prompt(x, lw["g_mix_pre"], lw["wm"], lw["wf"],
                                                     lw["bf"])
    ya = _pool_prompt(u, lw["wp"], lw["pool_scale"])
    eq, ek = _cumsum(lf, consts["tri"], consts["bias_lanes"])
    yb = _flash(q, kb, vb, eq, ek, tq=512)
    rw = lw["rw"] + (consts["seg"],)
    r, w, k2, vv, kn, bv, g, bonus = _rwkv_prep_prompt(pc_tm, rw)
    y, s_fin = _scan(r, w, k2, vv, kn, bv, None, consts["seg2"], consts["eye2"], tc=128)
    yc = _rwkv_post(y.reshape(n, WIDTH), bonus.reshape(n, WIDTH), g.reshape(n, WIDTH),
                    lw["lnw"], lw["lnb"], consts["seg"], BF16)
    yc = jnp.transpose(yc.reshape(t, b, WIDTH), (1, 0, 2)).reshape(n, WIDTH)
    x2 = _merge(x2, ya.reshape(n, WIDTH), yb.reshape(n, WIDTH), yc, lw["g_mix_pre"],
                lw["g_mix_post"], lw["wg"], lw["wa"], lw["wb"], lw["wc"], lw["wo"])
    x2 = _mlp(x2, lw["g_mlp_pre"], lw["g_mlp_post"], lw["wu"], lw["wd"])
    new = (k.reshape(b, t, NH, HD), v.reshape(b, t, NH, HD), lf[:, :, 0:NH],
           u[:, t - POOL_PREFIX:], pc_tm[t - 1], _tiles_to_state(s_fin))
    return x2.reshape(b, t, D_MODEL), new


def _sample_layer(x, lw, consts, l, cache_k, cache_v, cache_lf, state_pool, state_shift,
                  state_wkv, page_table):
    nb = x.shape[0]
    x2 = x.reshape(nb, D_MODEL)
    u, q, k, v, _, _, pc, lf = _in_proj(x2, lw["g_mix_pre"], lw["wm"], lw["wf"], lw["bf"], F32)
    ya = _pool_sample(u, jnp.transpose(state_pool, (1, 0, 2)), lw["wp"], lw["pool_scale"])
    lfn = jnp.broadcast_to(lf[:, 0:NH, None], (nb, NH, PAGE_SIZE))
    bias = _decode_bias(page_table, cache_lf, lfn, *consts["bias_mats"])
    yb = _decode_attn(page_table, q.reshape(nb, 1, WIDTH), k.reshape(nb, 1, WIDTH),
                      v.reshape(nb, 1, WIDTH), bias, consts["eye2"], cache_k, cache_v, l)
    rw = lw["rw"] + (consts["seg"],)
    r, w, k2, vv, kn, bv, g, bonus = _rwkv_prep_sample(pc, state_shift, rw)
    e3 = lambda a: a.reshape(1, nb, WIDTH)
    y, s_fin = _scan(e3(r), e3(w), e3(k2), e3(vv), e3(kn), e3(bv), _state_to_tiles(state_wkv),
                     consts["seg2"], consts["eye2"], tc=1)
    yc = _rwkv_post(y.reshape(nb, WIDTH), bonus, g, lw["lnw"], lw["lnb"], consts["seg"], F32)
    x2 = _merge(x2, ya, yb.reshape(nb, WIDTH), yc, lw["g_mix_pre"], lw["g_mix_post"], lw["wg"],
                lw["wa"], lw["wb"], lw["wc"], lw["wo"])
    x2 = _mlp(x2, lw["g_mlp_pre"], lw["g_mlp_post"], lw["wu"], lw["wd"])
    new = (k.reshape(nb, 1, NH, HD), v.reshape(nb, 1, NH, HD), lf[:, 0:NH].reshape(nb, 1, NH),
           jnp.concatenate([state_pool[:, 1:], u[:, None, :]], axis=1), pc,
           _tiles_to_state(s_fin))
    return x2.reshape(nb, 1, D_MODEL), new


def kernel(x_prompt, x_sample, cache_k, cache_v, cache_logf, state_pool, state_shift, state_wkv, page_table, norm_mix_pre, norm_mix_post, norm_mlp_pre, norm_mlp_post, w_in, b_forget, w_pool, pool_scale, rwkv_mu, rwkv_w0, rwkv_w_up, rwkv_a0, rwkv_a_up, rwkv_g_up, rwkv_k_k, rwkv_k_a, rwkv_r_k, rwkv_ln_w, rwkv_ln_b, w_branch_a, w_branch_b, w_branch_c, w_out, w_mlp_up, w_mlp_down):
    p = dict(norm_mix_pre=norm_mix_pre, norm_mix_post=norm_mix_post, norm_mlp_pre=norm_mlp_pre,
             norm_mlp_post=norm_mlp_post, w_in=w_in, b_forget=b_forget, w_pool=w_pool,
             pool_scale=pool_scale, rwkv_mu=rwkv_mu, rwkv_w0=rwkv_w0, rwkv_w_up=rwkv_w_up,
             rwkv_a0=rwkv_a0, rwkv_a_up=rwkv_a_up, rwkv_g_up=rwkv_g_up, rwkv_k_k=rwkv_k_k,
             rwkv_k_a=rwkv_k_a, rwkv_r_k=rwkv_r_k, rwkv_ln_w=rwkv_ln_w, rwkv_ln_b=rwkv_ln_b,
             w_branch_a=w_branch_a, w_branch_b=w_branch_b, w_branch_c=w_branch_c, w_out=w_out,
             w_mlp_up=w_mlp_up, w_mlp_down=w_mlp_down)
    depth, pool_pages = cache_k.shape[0], cache_k.shape[1]
    npages = page_table.shape[1]
    i = jnp.arange(CUM_BLK)
    eye = (jnp.arange(HD)[:, None] == (jnp.arange(LANES) % HD)[None, :]).astype(F32)
    consts = dict(seg=_seg_ones(WIDTH), seg2=_seg_ones(LANES), eye2=eye,
                  tri=(i[:, None] >= i[None, :]).astype(BF16),
                  bias_mats=_decode_bias_mats(npages), bias_lanes=_bias_lane_mats())
    ck = jnp.transpose(cache_k, (0, 1, 3, 4, 2))
    cv = jnp.transpose(cache_v, (0, 1, 3, 4, 2))
    clf = jnp.transpose(cache_logf, (0, 1, 3, 2))
    yp, ys = x_prompt, x_sample
    outs_p, outs_s = [], []
    for l in range(depth):
        lw = _layer_weights(l, p)
        yp, new_p = _prompt_layer(yp, lw, consts)
        outs_p.append(new_p)
        ys, new_s = _sample_layer(ys, lw, consts, l, ck, cv, clf[l], state_pool[l],
                                  state_shift[l], state_wkv[l], page_table)
        outs_s.append(new_s)
    stack = lambda outs: tuple(jnp.stack([o[j] for o in outs]) for j in range(6))
    return (yp, ys) + stack(outs_p) + stack(outs_s)
```

```python
import functools

import jax
import jax.numpy as jnp
from jax import lax
from jax.experimental import pallas as pl
from jax.experimental.pallas import tpu as pltpu

F32 = jnp.float32
BF16 = jnp.bfloat16

D_MODEL = 1024
DEPTH = 4
PAGE_SIZE = 128
POOL_WINDOWS = (2, 4, 8, 16)
A_WIDTH = 512
GROUP_W = 128
POOL_PREFIX = 15
HD = 64
NH = 8
WIDTH = 512
NHP = NH // 2
LANES = 128
NEG_INF = -1e30
DECAY_LORA = 64
AAA_LORA = 64
GATE_LORA = 128
C_IN = 3 * WIDTH + DECAY_LORA + AAA_LORA + GATE_LORA
LN_X_EPS = 64e-5
D_FF = 4 * D_MODEL
RMS_EPS = 1e-6
OFF_F = 4 * WIDTH
OFF_C = OFF_F + NH
OFF_G = OFF_C + C_IN
ATT_SCALE = HD ** -0.5
LOG2E = 1.4426950408889634
VMEM_LIMIT = 56 * 1024 * 1024


def _dot(a, b):
    return jnp.dot(a, b, preferred_element_type=F32)


def _dot_nt(a, b):
    return lax.dot_general(a, b, (((1,), (1,)), ((), ())), preferred_element_type=F32)


def _split2(x):
    hi = x.astype(BF16)
    lo = (x - hi.astype(F32)).astype(BF16)
    return hi, lo


def _dot_left2(x, m):
    hi, lo = _split2(x)
    return _dot(hi, m) + _dot(lo, m)


def _split3(x):
    hi = x.astype(BF16)
    r1 = x - hi.astype(F32)
    mid = r1.astype(BF16)
    lo = (r1 - mid.astype(F32)).astype(BF16)
    return hi, mid, lo


def _dot_left3(x, m):
    hi, mid, lo = _split3(x)
    return _dot(hi, m) + _dot(mid, m) + _dot(lo, m)


def _dot_right3(m, x):
    hi, mid, lo = _split3(x)
    return _dot(m, hi) + _dot(m, mid) + _dot(m, lo)


def _rms(x, g):
    ms = jnp.mean(x * x, axis=-1, keepdims=True)
    return x * lax.rsqrt(ms + RMS_EPS) * g


def _sigmoid(x):
    return 1.0 / (1.0 + jnp.exp(-x))


def _softplus(x):
    return jnp.maximum(x, 0.0) + jnp.log1p(jnp.exp(-jnp.abs(x)))


def _params(sem):
    return pltpu.CompilerParams(dimension_semantics=sem, vmem_limit_bytes=VMEM_LIMIT)


def _const_spec(shape):
    nd = len(shape)
    return pl.BlockSpec(shape, lambda *_: (0,) * nd, pipeline_mode=pl.Buffered(1))


def _row_tile(n, pref):
    return pref if n % pref == 0 else n


def _in_proj_kernel(x_ref, g_ref, wm_ref, wf_ref, bf_ref,
                    u_ref, q_ref, k_ref, v_ref, kb_ref, vb_ref, pc_ref, lf_ref):
    h = _rms(x_ref[...], g_ref[...]).astype(BF16)
    u_ref[...] = _dot(h, wm_ref[:, 0:WIDTH])
    q_ref[...] = (_dot(h, wm_ref[:, WIDTH:2 * WIDTH]) * ATT_SCALE).astype(q_ref.dtype)
    k = _dot(h, wm_ref[:, 2 * WIDTH:3 * WIDTH])
    k_ref[...] = k
    kb_ref[...] = k.astype(BF16)
    v = _dot(h, wm_ref[:, 3 * WIDTH:4 * WIDTH])
    v_ref[...] = v
    vb_ref[...] = v.astype(BF16)
    pc_ref[...] = _dot(h, wm_ref[:, 4 * WIDTH:4 * WIDTH + C_IN])
    f = _dot(h, wf_ref[...]) + bf_ref[...]
    lf_ref[...] = -_softplus(-f)


IN_TT = 64
SCAN_BB = 8
N_RW_WEIGHTS = 9


def _in_proj_prompt_kernel(x_ref, g_ref, wm_ref, wf_ref, bf_ref, perm_ref, *refs):
    rw_refs = refs[0:N_RW_WEIGHTS]
    u_ref, q_ref, k_ref, v_ref, kb_ref, vb_ref, lf_ref, pclast_ref = refs[N_RW_WEIGHTS:
                                                                           N_RW_WEIGHTS + 8]
    prep_refs = refs[N_RW_WEIGHTS + 8:-1]
    last_scr = refs[-1]
    nb, tt, _ = x_ref.shape
    rows = nb * tt

    @pl.when(pl.program_id(0) == 0)
    def _():
        last_scr[...] = jnp.zeros(last_scr.shape, F32)

    h = _rms(x_ref[...].reshape(rows, D_MODEL), g_ref[...]).astype(BF16)
    put = lambda ref, val: ref.__setitem__(Ellipsis, val.reshape(nb, tt, val.shape[-1]))
    put(u_ref, _dot(h, wm_ref[:, 0:WIDTH]))
    put(q_ref, (_dot(h, wm_ref[:, WIDTH:2 * WIDTH]) * (ATT_SCALE * LOG2E)).astype(q_ref.dtype))
    k = _dot(h, wm_ref[:, 2 * WIDTH:3 * WIDTH])
    put(k_ref, k)
    put(kb_ref, k.astype(BF16))
    v = _dot(h, wm_ref[:, 3 * WIDTH:4 * WIDTH])
    put(v_ref, v)
    put(vb_ref, v.astype(BF16))
    put(lf_ref, -_softplus(-(_dot(h, wf_ref[...]) + bf_ref[...])))
    h_tm = _dot(perm_ref[...], h).astype(BF16)
    pc = _dot(h_tm, wm_ref[:, 4 * WIDTH:4 * WIDTH + C_IN])
    prev = jnp.concatenate([last_scr[...], pc[0:rows - nb]], axis=0)
    last_scr[...] = pc[rows - nb:rows]
    pclast_ref[...] = pc[rows - nb:rows]
    for o_ref, val in zip(prep_refs, _rwkv_prep_math(pc, prev, *rw_refs)):
        o_ref[...] = val.reshape(tt, nb, WIDTH)


def _in_proj_prompt(x, g, wm, wf, bf, rw):
    nb, t, _ = x.shape
    assert nb == SCAN_BB and len(rw) == N_RW_WEIGHTS
    tt = _row_tile(t, IN_TT)
    r = jnp.arange(nb * tt)
    perm = ((r[:, None] % nb) * tt + r[:, None] // nb == r[None, :]).astype(BF16)
    bm = lambda w: pl.BlockSpec((nb, tt, w), lambda i: (0, i, 0))
    tm = pl.BlockSpec((tt, nb, WIDTH), lambda i: (i, 0, 0))
    sds = lambda w, dt: jax.ShapeDtypeStruct((nb, t, w), dt)
    return pl.pallas_call(
        _in_proj_prompt_kernel,
        grid=(t // tt,),
        in_specs=[bm(D_MODEL), _const_spec((1, D_MODEL)), _const_spec(wm.shape),
                  _const_spec(wf.shape), _const_spec((1, LANES)), _const_spec(perm.shape)]
                 + [_const_spec(w.shape) for w in rw],
        out_specs=(bm(WIDTH), bm(WIDTH), bm(WIDTH), bm(WIDTH), bm(WIDTH), bm(WIDTH), bm(LANES),
                   pl.BlockSpec((nb, C_IN), lambda i: (0, 0))) + (tm,) * 8,
        out_shape=(sds(WIDTH, F32), sds(WIDTH, BF16), sds(WIDTH, F32), sds(WIDTH, F32),
                   sds(WIDTH, BF16), sds(WIDTH, BF16), sds(LANES, F32),
                   jax.ShapeDtypeStruct((nb, C_IN), F32))
                  + tuple(jax.ShapeDtypeStruct((t, nb, WIDTH), F32) for _ in range(8)),
        scratch_shapes=[pltpu.VMEM((nb, C_IN), F32)],
        compiler_params=_params(("arbitrary",)),
        name="in_proj_prompt",
    )(x, g, wm, wf, bf, perm, *rw)


def _in_proj(x, g, wm, wf, bf, q_dtype):
    n = x.shape[0]
    tm = _row_tile(n, 512)
    row = lambda w: pl.BlockSpec((tm, w), lambda i: (i, 0))
    out_shape = (
        jax.ShapeDtypeStruct((n, WIDTH), F32),
        jax.ShapeDtypeStruct((n, WIDTH), q_dtype),
        jax.ShapeDtypeStruct((n, WIDTH), F32),
        jax.ShapeDtypeStruct((n, WIDTH), F32),
        jax.ShapeDtypeStruct((n, WIDTH), BF16),
        jax.ShapeDtypeStruct((n, WIDTH), BF16),
        jax.ShapeDtypeStruct((n, C_IN), F32),
        jax.ShapeDtypeStruct((n, LANES), F32),
    )
    return pl.pallas_call(
        _in_proj_kernel,
        grid=(n // tm,),
        in_specs=[row(D_MODEL), _const_spec((1, D_MODEL)), _const_spec(wm.shape),
                  _const_spec(wf.shape), _const_spec((1, LANES))],
        out_specs=(row(WIDTH), row(WIDTH), row(WIDTH), row(WIDTH), row(WIDTH), row(WIDTH),
                   row(C_IN), row(LANES)),
        out_shape=out_shape,
        compiler_params=_params(("parallel",)),
        name="in_proj",
    )(x, g, wm, wf, bf)


CUM_BLK = 256


BIAS_TERMS = 3


def _cumsum_kernel(lf_ref, tri_ref, pq_ref, pk_ref, oq_ref, ok_ref, eq_ref, ek_ref):
    t = lf_ref.shape[1]
    carry = jnp.zeros((1, LANES), F32)
    for i in range(t // CUM_BLK):
        sl = slice(i * CUM_BLK, (i + 1) * CUM_BLK)
        c = _dot_right3(tri_ref[...], lf_ref[0, sl, :]) + carry
        terms = jnp.concatenate(_split3(c * LOG2E), axis=1)
        eq_ref[0, sl, :] = (_dot(terms, pq_ref[...]) + oq_ref[...]).astype(BF16)
        ek_ref[0, sl, :] = (ok_ref[...] - _dot(terms, pk_ref[...])).astype(BF16)
        carry = c[CUM_BLK - 1:CUM_BLK, :]


def _bias_lane_mats():
    src = jnp.arange(BIAS_TERMS * LANES)
    part, head = src // LANES, src % LANES
    dst = jnp.arange(NH * LANES)
    dhead, dlane = dst // LANES, dst % LANES
    other = HD * (1 - dhead % 2)
    slot = dlane - other
    same = head[:, None] == dhead[None, :]
    pq = (same & (slot[None, :] == part[:, None])).astype(BF16)
    pk = (same & (slot[None, :] == BIAS_TERMS + part[:, None])).astype(BF16)
    oq = ((slot >= BIAS_TERMS) & (slot < 2 * BIAS_TERMS)).astype(F32).reshape(1, -1)
    ok = ((slot >= 0) & (slot < BIAS_TERMS)).astype(F32).reshape(1, -1)
    return pq, pk, oq, ok


def _cumsum(lf, tri, mats):
    b, t, _ = lf.shape
    out = pl.BlockSpec((1, t, NH * LANES), lambda i: (i, 0, 0))
    return pl.pallas_call(
        _cumsum_kernel,
        grid=(b,),
        in_specs=[pl.BlockSpec((1, t, LANES), lambda i: (i, 0, 0)), _const_spec(tri.shape)]
                 + [_const_spec(m.shape) for m in mats],
        out_specs=(out, out),
        out_shape=(jax.ShapeDtypeStruct((b, t, NH * LANES), BF16),
                   jax.ShapeDtypeStruct((b, t, NH * LANES), BF16)),
        compiler_params=_params(("parallel",)),
        name="fox_cumsum",
    )(lf, tri, *mats)


def _flash_kernel(q_ref, k_ref, v_ref, eq_ref, ek_ref, o_ref, m_scr, l_scr, acc_scr, *, tq, tk):
    qi = pl.program_id(1)
    ki = pl.program_id(2)

    @pl.when(ki == 0)
    def _():
        m_scr[...] = jnp.full(m_scr.shape, NEG_INF, F32)
        l_scr[...] = jnp.zeros(l_scr.shape, F32)
        acc_scr[...] = jnp.zeros(acc_scr.shape, F32)

    def block(on_diagonal):
        lane = lax.broadcasted_iota(jnp.int32, (1, LANES), 1)
        own = [((lane // HD) == j).astype(F32).astype(BF16) for j in range(2)]
        if on_diagonal:
            causal = (lax.broadcasted_iota(jnp.int32, (tq, tk), 1)
                      <= lax.broadcasted_iota(jnp.int32, (tq, tk), 0))
        for hp in range(NHP):
            ls = slice(hp * LANES, (hp + 1) * LANES)
            q = q_ref[0, :, ls]
            k = k_ref[0, :, ls]
            v = v_ref[0, :, ls]
            pv = []
            alpha = []
            for j in range(2):
                h = 2 * hp + j
                hs = slice(h * LANES, (h + 1) * LANES)
                s = _dot_nt(q * own[j] + eq_ref[0, :, hs], k * own[j] + ek_ref[0, :, hs])
                if on_diagonal:
                    s = jnp.where(causal, s, NEG_INF)
                m_prev = m_scr[h]
                m_new = jnp.maximum(m_prev, jnp.max(s, axis=1, keepdims=True))
                a = jnp.exp2(m_prev - m_new)
                p = jnp.exp2(s - m_new[:, 0:1])
                l_scr[h] = a * l_scr[h] + jnp.sum(p, axis=1, keepdims=True)
                m_scr[h] = m_new
                pv.append(_dot(p.astype(BF16), v))
                alpha.append(a)
            first = (lane // HD) == 0
            acc_scr[hp] = (jnp.where(first, alpha[0], alpha[1]) * acc_scr[hp]
                           + jnp.where(first, pv[0], pv[1]))

    pl.when(ki < qi)(functools.partial(block, False))
    pl.when(ki == qi)(functools.partial(block, True))

    @pl.when(ki == qi)
    def _():
        lane = lax.broadcasted_iota(jnp.int32, (1, LANES), 1)
        for hp in range(NHP):
            l = jnp.where((lane // HD) == 0, l_scr[2 * hp], l_scr[2 * hp + 1])
            o_ref[0, :, hp * LANES:(hp + 1) * LANES] = (acc_scr[hp] / l).astype(o_ref.dtype)


def _flash(q, k, v, eq, ek, *, tq):
    b, t, _ = q.shape
    tq = min(tq, t)
    nq = t // tq
    kv_idx = lambda bi, qi, ki: (bi, jnp.minimum(ki, qi), 0)
    return pl.pallas_call(
        functools.partial(_flash_kernel, tq=tq, tk=tq),
        grid=(b, nq, nq),
        in_specs=[pl.BlockSpec((1, tq, WIDTH), lambda bi, qi, ki: (bi, qi, 0)),
                  pl.BlockSpec((1, tq, WIDTH), kv_idx),
                  pl.BlockSpec((1, tq, WIDTH), kv_idx),
                  pl.BlockSpec((1, tq, NH * LANES), lambda bi, qi, ki: (bi, qi, 0)),
                  pl.BlockSpec((1, tq, NH * LANES), kv_idx)],
        out_specs=pl.BlockSpec((1, tq, WIDTH), lambda bi, qi, ki: (bi, qi, 0)),
        out_shape=jax.ShapeDtypeStruct((b, t, WIDTH), BF16),
        scratch_shapes=[pltpu.VMEM((NH, tq, LANES), F32), pltpu.VMEM((NH, tq, LANES), F32),
                        pltpu.VMEM((NHP, tq, LANES), F32)],
        compiler_params=_params(("parallel", "parallel", "arbitrary")),
        name="fox_flash",
    )(q, k, v, eq, ek)


def _decode_bias_kernel(pt_ref, tbl_ref, lfn_ref, msuf_ref, mtot_ref, plater_ref, o_ref, g_scr):
    b = pl.program_id(0)
    npages = g_scr.shape[0]
    for p in range(npages):
        g_scr[p] = tbl_ref[pt_ref[b, p]]
    g = g_scr[...].reshape(npages * NH, PAGE_SIZE)
    within = _dot_left3(g, msuf_ref[...])
    tot = _dot_left3(g, mtot_ref[...])
    later = _dot_right3(plater_ref[...], tot)
    o_ref[0] = (within + later).reshape(npages, NH, PAGE_SIZE) + lfn_ref[...]


def _decode_bias(page_table, tbl, lfn, msuf, mtot, plater):
    nb, npages = page_table.shape
    grid_spec = pltpu.PrefetchScalarGridSpec(
        num_scalar_prefetch=1,
        grid=(nb,),
        in_specs=[_const_spec(tbl.shape),
                  pl.BlockSpec((1, NH, PAGE_SIZE), lambda b, pt: (b, 0, 0)),
                  _const_spec(msuf.shape), _const_spec(mtot.shape), _const_spec(plater.shape)],
        out_specs=pl.BlockSpec((1, npages, NH, PAGE_SIZE), lambda b, pt: (b, 0, 0, 0)),
        scratch_shapes=[pltpu.VMEM((npages, NH, PAGE_SIZE), F32)],
    )
    return pl.pallas_call(
        _decode_bias_kernel,
        grid_spec=grid_spec,
        out_shape=jax.ShapeDtypeStruct((nb, npages, NH, PAGE_SIZE), F32),
        compiler_params=_params(("arbitrary",)),
        name="fox_decode_bias",
    )(page_table, tbl, lfn, msuf, mtot, plater)


PAGES_PER_STEP = 16


def _col_bcast(row_pair, eye2, j):
    lane = lax.broadcasted_iota(jnp.int32, (1, LANES), 1)
    diag = jnp.where((lane // HD) == j, jnp.broadcast_to(row_pair, (HD, LANES)) * eye2, 0.0)
    return jnp.broadcast_to(jnp.sum(diag, axis=1, keepdims=True), (HD, LANES))


def _decode_attn_kernel(pt_ref, q_ref, kn_ref, vn_ref, bias_ref, eye_ref, *refs):
    k_refs = refs[0:PAGES_PER_STEP]
    v_refs = refs[PAGES_PER_STEP:2 * PAGES_PER_STEP]
    o_ref, m_scr, l_scr, acc_scr, qc_scr = refs[2 * PAGES_PER_STEP:]
    g = pl.program_id(1)
    eye2 = eye_ref[...]
    lane = lax.broadcasted_iota(jnp.int32, (1, LANES), 1)
    sub = lax.broadcasted_iota(jnp.int32, (NH, PAGE_SIZE), 0)

    @pl.when(g == 0)
    def _():
        q = q_ref[0]
        own = (lax.broadcasted_iota(jnp.int32, (NH, WIDTH), 1) // HD
               == lax.broadcasted_iota(jnp.int32, (NH, WIDTH), 0))
        qk = jnp.broadcast_to(q * kn_ref[0], (NH, WIDTH))
        s_new = jnp.sum(jnp.where(own, qk, 0.0), axis=1, keepdims=True)
        m_scr[...] = jnp.broadcast_to(s_new, m_scr.shape)
        l_scr[...] = jnp.ones(l_scr.shape, F32)
        vn = vn_ref[0]
        for h in range(NH):
            ls = slice((h // 2) * LANES, (h // 2 + 1) * LANES)
            qc_scr[h] = _col_bcast(q[:, ls], eye2, h % 2)
            acc_scr[h] = jnp.where(lane == 0, _col_bcast(vn[:, ls], eye2, h % 2), 0.0)

    s_parts = []
    for j in range(PAGES_PER_STEP):
        s_page = jnp.zeros((NH, PAGE_SIZE), F32)
        for h in range(NH):
            sh = jnp.sum(k_refs[j][0, 0, h] * qc_scr[h], axis=0, keepdims=True)
            s_page = jnp.where(sub == h, jnp.broadcast_to(sh, (NH, PAGE_SIZE)), s_page)
        s_parts.append(s_page + bias_ref[0, j])
    s = jnp.concatenate(s_parts, axis=1)
    m_prev = m_scr[...]
    m_new = jnp.maximum(m_prev, jnp.max(s, axis=1, keepdims=True))
    a = jnp.exp(m_prev - m_new)
    p = jnp.exp(s - m_new[:, 0:1])
    l_scr[...] = a * l_scr[...] + jnp.sum(p, axis=1, keepdims=True)
    m_scr[...] = m_new
    for h in range(NH):
        acc = acc_scr[h] * a[h:h + 1, 0:1]
        for j in range(PAGES_PER_STEP):
            ph = p[h:h + 1, j * PAGE_SIZE:(j + 1) * PAGE_SIZE]
            acc = acc + v_refs[j][0, 0, h] * jnp.broadcast_to(ph, (HD, PAGE_SIZE))
        acc_scr[h] = acc

    @pl.when(g == pl.num_programs(1) - 1)
    def _():
        rows = []
        for hp in range(NHP):
            pair = []
            for j in range(2):
                h = 2 * hp + j
                col = jnp.sum(acc_scr[h], axis=1, keepdims=True) / l_scr[h:h + 1, 0:1]
                pair.append(jnp.sum(jnp.broadcast_to(col, (HD, LANES)) * eye2, axis=0,
                                    keepdims=True))
            rows.append(jnp.where((lane // HD) == 0, pair[0], pair[1]))
        o_ref[0] = jnp.concatenate(rows, axis=1)


def _decode_attn(page_table, q, k_new, v_new, bias, eye2, cache_kt, cache_vt, layer):
    nb, npages = page_table.shape
    ng = npages // PAGES_PER_STEP

    def page_spec(j):
        return pl.BlockSpec((1, 1, NH, HD, PAGE_SIZE),
                            lambda b, g, pt: (layer, pt[b, g * PAGES_PER_STEP + j], 0, 0, 0))

    vec = lambda: pl.BlockSpec((1, 1, WIDTH), lambda b, g, pt: (b, 0, 0))
    grid_spec = pltpu.PrefetchScalarGridSpec(
        num_scalar_prefetch=1,
        grid=(nb, ng),
        in_specs=[vec(), vec(), vec(),
                  pl.BlockSpec((1, PAGES_PER_STEP, NH, PAGE_SIZE), lambda b, g, pt: (b, g, 0, 0)),
                  _const_spec(eye2.shape)]
                 + [page_spec(j) for j in range(PAGES_PER_STEP)]
                 + [page_spec(j) for j in range(PAGES_PER_STEP)],
        out_specs=pl.BlockSpec((1, 1, WIDTH), lambda b, g, pt: (b, 0, 0)),
        scratch_shapes=[pltpu.VMEM((NH, LANES), F32), pltpu.VMEM((NH, LANES), F32),
                        pltpu.VMEM((NH, HD, PAGE_SIZE), F32), pltpu.VMEM((NH, HD, LANES), F32)],
    )
    return pl.pallas_call(
        _decode_attn_kernel,
        grid_spec=grid_spec,
        out_shape=jax.ShapeDtypeStruct((nb, 1, WIDTH), F32),
        compiler_params=_params(("parallel", "arbitrary")),
        name="fox_decode_attn",
    )(page_table, q, k_new, v_new, bias, eye2,
      *([cache_kt] * PAGES_PER_STEP), *([cache_vt] * PAGES_PER_STEP))


POOL_HALO = 16


def _pool_project(pooled, wp_ref, sc_ref):
    outs = []
    for g in range(len(POOL_WINDOWS)):
        ls = slice(g * GROUP_W, (g + 1) * GROUP_W)
        outs.append(_dot(pooled[g].astype(BF16), wp_ref[g]) * sc_ref[:, ls])
    return outs


def _pool_prompt_kernel(u_ref, wp_ref, sc_ref, o_ref, ext_scr, *, tt):
    ti = pl.program_id(1)

    @pl.when(ti == 0)
    def _():
        ext_scr[0:POOL_HALO, :] = jnp.zeros((POOL_HALO, A_WIDTH), F32)

    ext_scr[POOL_HALO:POOL_HALO + tt, :] = u_ref[0]
    pos1 = ti * tt + lax.broadcasted_iota(jnp.int32, (tt, GROUP_W), 0) + 1
    pooled = []
    for g, w in enumerate(POOL_WINDOWS):
        ls = slice(g * GROUP_W, (g + 1) * GROUP_W)
        x = ext_scr[POOL_HALO:POOL_HALO + tt, ls]
        win = x
        for d in range(1, w):
            win = win + ext_scr[POOL_HALO - d:POOL_HALO - d + tt, ls]
        cnt = jnp.minimum(w, pos1).astype(F32)
        pooled.append(win / cnt - x)
    outs = _pool_project(pooled, wp_ref, sc_ref)
    for g in range(len(POOL_WINDOWS)):
        o_ref[0, :, g * GROUP_W:(g + 1) * GROUP_W] = outs[g].astype(o_ref.dtype)
    ext_scr[0:POOL_HALO, :] = ext_scr[tt:tt + POOL_HALO, :]


def _pool_prompt(u, wp, sc):
    b, t, _ = u.shape
    tt = _row_tile(t, 512)
    return pl.pallas_call(
        functools.partial(_pool_prompt_kernel, tt=tt),
        grid=(b, t // tt),
        in_specs=[pl.BlockSpec((1, tt, A_WIDTH), lambda bi, ti: (bi, ti, 0)),
                  _const_spec(wp.shape), _const_spec((1, A_WIDTH))],
        out_specs=pl.BlockSpec((1, tt, A_WIDTH), lambda bi, ti: (bi, ti, 0)),
        out_shape=jax.ShapeDtypeStruct((b, t, A_WIDTH), BF16),
        scratch_shapes=[pltpu.VMEM((POOL_HALO + tt, A_WIDTH), F32)],
        compiler_params=_params(("parallel", "arbitrary")),
        name="pool_prompt",
    )(u, wp, sc)


def _pool_sample_kernel(u_ref, prev_ref, wp_ref, sc_ref, o_ref):
    pooled = []
    for g, w in enumerate(POOL_WINDOWS):
        ls = slice(g * GROUP_W, (g + 1) * GROUP_W)
        x = u_ref[:, ls]
        win = x
        for d in range(1, w):
            win = win + prev_ref[POOL_PREFIX - d, :, ls]
        pooled.append(win / float(w) - x)
    outs = _pool_project(pooled, wp_ref, sc_ref)
    for g in range(len(POOL_WINDOWS)):
        o_ref[:, g * GROUP_W:(g + 1) * GROUP_W] = outs[g]


def _pool_sample(u, prev_t, wp, sc):
    nb = u.shape[0]
    return pl.pallas_call(
        _pool_sample_kernel,
        out_shape=jax.ShapeDtypeStruct((nb, A_WIDTH), F32),
        compiler_params=pltpu.CompilerParams(vmem_limit_bytes=VMEM_LIMIT),
        name="pool_sample",
    )(u, prev_t, wp, sc)


def _rwkv_prep_math(pc, prev, mu_ref, wlo_ref, w0_ref, a0_ref, gup_ref, kk_ref, ka_ref, rk_ref,
                    seg_ref):
    xs = pc + (prev - pc) * mu_ref[...]
    r = xs[:, 0:WIDTH]
    k = xs[:, WIDTH:2 * WIDTH]
    v = xs[:, 2 * WIDTH:3 * WIDTH]
    lora = xs[:, 3 * WIDTH:3 * WIDTH + LANES]
    gd = xs[:, 3 * WIDTH + LANES:3 * WIDTH + 2 * LANES]
    lane = lax.broadcasted_iota(jnp.int32, (1, LANES), 1)
    lhs = jnp.where(lane < DECAY_LORA, jnp.tanh(lora), lora).astype(BF16)
    up = _dot(lhs, wlo_ref[...])
    w = -_softplus(-(w0_ref[...] + up[:, 0:WIDTH])) - 0.5
    decay = jnp.exp(-jnp.exp(w))
    a = _sigmoid(a0_ref[...] + up[:, WIDTH:2 * WIDTH])
    g = _dot(_sigmoid(gd).astype(BF16), gup_ref[...])
    kk = k * kk_ref[...]
    nrm = jnp.sqrt(_dot_left2(kk * kk, seg_ref[...]))
    kk = kk / jnp.maximum(nrm, 1e-12)
    k2 = k * (1.0 + (a - 1.0) * ka_ref[...])
    bonus = _dot_left2(r * k2 * rk_ref[...], seg_ref[...]) * v
    return r, decay, k2, v, -kk, kk * a, g, bonus


def _rwkv_prep_sample_kernel(pc_ref, prev_ref, mu_ref, wlo_ref, w0_ref, a0_ref, gup_ref, kk_ref,
                             ka_ref, rk_ref, seg_ref, *out_refs):
    outs = _rwkv_prep_math(pc_ref[...], prev_ref[...], mu_ref, wlo_ref, w0_ref, a0_ref, gup_ref,
                           kk_ref, ka_ref, rk_ref, seg_ref)
    for o_ref, val in zip(out_refs, outs):
        o_ref[...] = val


def _rwkv_prep_sample(pc, prev, weights):
    nb = pc.shape[0]
    return pl.pallas_call(
        _rwkv_prep_sample_kernel,
        out_shape=tuple(jax.ShapeDtypeStruct((nb, WIDTH), F32) for _ in range(8)),
        compiler_params=pltpu.CompilerParams(vmem_limit_bytes=VMEM_LIMIT),
        name="rwkv_prep_sample",
    )(pc, prev, *weights)


def _scan_kernel(r_ref, w_ref, k_ref, v_ref, kn_ref, b_ref, s0_ref, seg_ref, eye_ref,
                 y_ref, so_ref, s_scr, *, tc, zero_init):
    ci = pl.program_id(1)

    @pl.when(ci == 0)
    def _():
        if zero_init:
            s_scr[...] = jnp.zeros(s_scr.shape, F32)
        else:
            s_scr[...] = s0_ref[...]

    seg = seg_ref[...]
    eye2 = eye_ref[...]
    groups = [(b, hp) for b in range(SCAN_BB) for hp in range(NHP)]
    ng = len(groups)
    sub = lax.broadcasted_iota(jnp.int32, (SCAN_BB, LANES), 0)

    def row(slab, b, hp):
        return jnp.broadcast_to(slab[b:b + 1, hp * LANES:(hp + 1) * LANES], (HD, LANES))

    def row16(slab, b, hp):
        return row(slab, b, hp).astype(BF16)

    def seg_sum(parts):
        return _dot(jnp.concatenate(parts, axis=0), seg)

    eye16 = eye2.astype(BF16)

    def write_y(yb, t):
        y_tiles = [jnp.zeros((SCAN_BB, LANES), F32) for _ in range(NHP)]
        for n, (b, hp) in enumerate(groups):
            y_row = jnp.sum(yb[n * HD:(n + 1) * HD] * eye2, axis=0, keepdims=True)
            y_tiles[hp] = jnp.where(sub == b, jnp.broadcast_to(y_row, (SCAN_BB, LANES)),
                                    y_tiles[hp])
        y_ref[t] = jnp.concatenate(y_tiles, axis=1)

    def step(t, carry):
        tp = jnp.maximum(t - 1, 0)
        w_t, k_t, v_t, kn_t, b_t = (ref[t] for ref in (w_ref, k_ref, v_ref, kn_ref, b_ref))
        r_p = r_ref[tp]
        s16 = [s_scr[b, hp].astype(BF16) for b, hp in groups]
        res = seg_sum([s16[n] * row16(kn_t, b, hp) for n, (b, hp) in enumerate(groups)]
                      + [eye16 * row16(v_t, b, hp) for b, hp in groups]
                      + [s16[n] * row16(r_p, b, hp) for n, (b, hp) in enumerate(groups)])
        for n, (b, hp) in enumerate(groups):
            sa = res[n * HD:(n + 1) * HD]
            vb = res[(ng + n) * HD:(ng + n + 1) * HD]
            s_scr[b, hp] = (s_scr[b, hp] * row(w_t, b, hp) + sa * row(b_t, b, hp)
                            + vb * row(k_t, b, hp))
        write_y(res[2 * ng * HD:], tp)
        return carry

    lax.fori_loop(0, tc, step, 0)
    r_last = r_ref[tc - 1]
    write_y(seg_sum([s_scr[b, hp].astype(BF16) * row16(r_last, b, hp) for b, hp in groups]),
            tc - 1)

    @pl.when(ci == pl.num_programs(1) - 1)
    def _():
        so_ref[...] = s_scr[...]


def _scan(r, w, k, v, kn, bv, s0, seg2, eye2, *, tc):
    t, nb, _ = r.shape
    tc = min(tc, t)
    zero_init = s0 is None
    if zero_init:
        s0 = jnp.zeros((SCAN_BB, NHP, HD, LANES), F32)
        s0_spec = pl.BlockSpec((SCAN_BB, NHP, HD, LANES), lambda bi, ci: (0, 0, 0, 0))
    else:
        s0_spec = pl.BlockSpec((SCAN_BB, NHP, HD, LANES), lambda bi, ci: (bi, 0, 0, 0))
    seq = lambda: pl.BlockSpec((tc, SCAN_BB, WIDTH), lambda bi, ci: (ci, bi, 0))
    return pl.pallas_call(
        functools.partial(_scan_kernel, tc=tc, zero_init=zero_init),
        grid=(nb // SCAN_BB, t // tc),
        in_specs=[seq(), seq(), seq(), seq(), seq(), seq(), s0_spec,
                  _const_spec(seg2.shape), _const_spec(eye2.shape)],
        out_specs=(seq(), pl.BlockSpec((SCAN_BB, NHP, HD, LANES), lambda bi, ci: (bi, 0, 0, 0))),
        out_shape=(jax.ShapeDtypeStruct((t, nb, WIDTH), F32),
                   jax.ShapeDtypeStruct((nb, NHP, HD, LANES), F32)),
        scratch_shapes=[pltpu.VMEM((SCAN_BB, NHP, HD, LANES), F32)],
        compiler_params=_params(("parallel", "arbitrary")),
        name="rwkv_scan",
    )(r, w, k, v, kn, bv, s0, seg2, eye2)


def _rwkv_post_kernel(y_ref, bonus_ref, g_ref, lnw_ref, lnb_ref, seg_ref, o_ref):
    y = y_ref[...]
    seg = seg_ref[...]
    mean = _dot_left2(y, seg) * (1.0 / HD)
    d = y - mean
    var = _dot_left2(d * d, seg) * (1.0 / HD)
    yn = d * lax.rsqrt(var + LN_X_EPS) * lnw_ref[...] + lnb_ref[...]
    o_ref[...] = ((yn + bonus_ref[...]) * g_ref[...]).astype(o_ref.dtype)


def _rwkv_post(y, bonus, g, lnw, lnb, seg, out_dtype):
    n = y.shape[0]
    tm = _row_tile(n, 512)
    row = pl.BlockSpec((tm, WIDTH), lambda i: (i, 0))
    return pl.pallas_call(
        _rwkv_post_kernel,
        grid=(n // tm,),
        in_specs=[row, row, row, _const_spec((1, WIDTH)), _const_spec((1, WIDTH)),
                  _const_spec(seg.shape)],
        out_specs=row,
        out_shape=jax.ShapeDtypeStruct((n, WIDTH), out_dtype),
        compiler_params=_params(("parallel",)),
        name="rwkv_post",
    )(y, bonus, g, lnw, lnb, seg)


def _merge_kernel(x_ref, ya_ref, yb_ref, yc_ref, gpre_ref, gpost_ref, wg_ref, wa_ref, wb_ref,
                  wc_ref, wo_ref, o_ref):
    x = x_ref[...]
    h = _rms(x, gpre_ref[...]).astype(BF16)
    merged = None
    for n, (y_ref, w_ref) in enumerate(((ya_ref, wa_ref), (yb_ref, wb_ref), (yc_ref, wc_ref))):
        gate = _sigmoid(_dot(h, wg_ref[:, n * D_MODEL:(n + 1) * D_MODEL]))
        term = gate * _dot(y_ref[...].astype(BF16), w_ref[...])
        merged = term if merged is None else merged + term
    o = _dot(merged.astype(BF16), wo_ref[...])
    o_ref[...] = x + _rms(o, gpost_ref[...])


def _merge(x, ya, yb, yc, gpre, gpost, wg, wa, wb, wc, wo):
    n = x.shape[0]
    tm = _row_tile(n, 512)
    row = lambda w: pl.BlockSpec((tm, w), lambda i: (i, 0))
    return pl.pallas_call(
        _merge_kernel,
        grid=(n // tm,),
        in_specs=[row(D_MODEL), row(WIDTH), row(WIDTH), row(WIDTH),
                  _const_spec((1, D_MODEL)), _const_spec((1, D_MODEL)),
                  _const_spec(wg.shape), _const_spec(wa.shape), _const_spec(wb.shape),
                  _const_spec(wc.shape), _const_spec(wo.shape)],
        out_specs=row(D_MODEL),
        out_shape=jax.ShapeDtypeStruct((n, D_MODEL), F32),
        compiler_params=_params(("parallel",)),
        name="merge",
    )(x, ya, yb, yc, gpre, gpost, wg, wa, wb, wc, wo)


def _mlp_kernel(x_ref, gpre_ref, gpost_ref, wu_ref, wd_ref, o_ref):
    x = x_ref[...]
    h = _rms(x, gpre_ref[...]).astype(BF16)
    up = jnp.maximum(_dot(h, wu_ref[...]), 0.0)
    down = _dot((up * up).astype(BF16), wd_ref[...])
    o_ref[...] = x + _rms(down, gpost_ref[...])


def _mlp(x, gpre, gpost, wu, wd):
    n = x.shape[0]
    tm = _row_tile(n, 512)
    row = pl.BlockSpec((tm, D_MODEL), lambda i: (i, 0))
    return pl.pallas_call(
        _mlp_kernel,
        grid=(n // tm,),
        in_specs=[row, _const_spec((1, D_MODEL)), _const_spec((1, D_MODEL)),
                  _const_spec(wu.shape), _const_spec(wd.shape)],
        out_specs=row,
        out_shape=jax.ShapeDtypeStruct((n, D_MODEL), F32),
        compiler_params=_params(("parallel",)),
        name="mlp",
    )(x, gpre, gpost, wu, wd)


def _seg_ones(n):
    i = jnp.arange(n)
    return (i[:, None] // HD == i[None, :] // HD).astype(BF16)


def _decode_bias_mats(npages):
    pos = jnp.arange(PAGE_SIZE)
    msuf = (pos[:, None] > pos[None, :]).astype(BF16)
    mtot = jnp.ones((PAGE_SIZE, PAGE_SIZE), BF16)
    r = jnp.arange(npages * NH)
    plater = ((r[None, :] // NH > r[:, None] // NH)
              & (r[None, :] % NH == r[:, None] % NH)).astype(BF16)
    return msuf, mtot, plater


def _state_to_tiles(s):
    nb = s.shape[0]
    return s.reshape(nb, NHP, 2, HD, HD).transpose(0, 1, 3, 2, 4).reshape(nb, NHP, HD, LANES)


def _tiles_to_state(s):
    nb = s.shape[0]
    return s.reshape(nb, NHP, HD, 2, HD).transpose(0, 1, 3, 2, 4).reshape(nb, NH, HD, HD)


def _layer_weights(l, p):
    w_in = p["w_in"][l]
    row = lambda a: a[l].reshape(1, -1)
    wlo = jnp.zeros((LANES, 2 * WIDTH), F32)
    wlo = wlo.at[0:DECAY_LORA, 0:WIDTH].set(p["rwkv_w_up"][l])
    wlo = wlo.at[DECAY_LORA:, WIDTH:].set(p["rwkv_a_up"][l])
    return dict(
        g_mix_pre=row(p["norm_mix_pre"]), g_mix_post=row(p["norm_mix_post"]),
        g_mlp_pre=row(p["norm_mlp_pre"]), g_mlp_post=row(p["norm_mlp_post"]),
        wm=jnp.concatenate([w_in[:, 0:OFF_F], w_in[:, OFF_C:OFF_G]], axis=1).astype(BF16),
        wf=jnp.pad(w_in[:, OFF_F:OFF_C], ((0, 0), (0, LANES - NH))).astype(BF16),
        bf=jnp.pad(p["b_forget"][l], (0, LANES - NH)).reshape(1, LANES),
        wg=w_in[:, OFF_G:].astype(BF16),
        wp=p["w_pool"][l].astype(BF16), pool_scale=row(p["pool_scale"]),
        rw=(row(p["rwkv_mu"]), wlo.astype(BF16), row(p["rwkv_w0"]), row(p["rwkv_a0"]),
            p["rwkv_g_up"][l].astype(BF16), row(p["rwkv_k_k"]), row(p["rwkv_k_a"]),
            row(p["rwkv_r_k"])),
        lnw=row(p["rwkv_ln_w"]), lnb=row(p["rwkv_ln_b"]),
        wa=p["w_branch_a"][l].astype(BF16), wb=p["w_branch_b"][l].astype(BF16),
        wc=p["w_branch_c"][l].astype(BF16), wo=p["w_out"][l].astype(BF16),
        wu=p["w_mlp_up"][l].astype(BF16), wd=p["w_mlp_down"][l].astype(BF16),
    )


def _prompt_layer(x, lw, consts):
    b, t, _ = x.shape
    n = b * t
    x2 = x.reshape(n, D_MODEL)
    (u, q, k, v, kb, vb, lf, pc_last, r, w, k2, vv, kn, bv, g, bonus) = _in_proj_prompt(
        x, lw["g_mix_pre"], lw["wm"], lw["wf"], lw["bf"], lw["rw"] + (consts["seg"],))
    ya = _pool_prompt(u, lw["wp"], lw["pool_scale"])
    eq, ek = _cumsum(lf, consts["tri"], consts["bias_lanes"])
    yb = _flash(q, kb, vb, eq, ek, tq=512)
    y, s_fin = _scan(r, w, k2, vv, kn, bv, None, consts["seg2"], consts["eye2"], tc=128)
    yc = _rwkv_post(y.reshape(n, WIDTH), bonus.reshape(n, WIDTH), g.reshape(n, WIDTH),
                    lw["lnw"], lw["lnb"], consts["seg"], BF16)
    yc = jnp.transpose(yc.reshape(t, b, WIDTH), (1, 0, 2)).reshape(n, WIDTH)
    x2 = _merge(x2, ya.reshape(n, WIDTH), yb.reshape(n, WIDTH), yc, lw["g_mix_pre"],
                lw["g_mix_post"], lw["wg"], lw["wa"], lw["wb"], lw["wc"], lw["wo"])
    x2 = _mlp(x2, lw["g_mlp_pre"], lw["g_mlp_post"], lw["wu"], lw["wd"])
    new = (k.reshape(b, t, NH, HD), v.reshape(b, t, NH, HD), lf[:, :, 0:NH],
           u[:, t - POOL_PREFIX:], pc_last, _tiles_to_state(s_fin))
    return x2.reshape(b, t, D_MODEL), new


def _sample_layer(x, lw, consts, l, cache_k, cache_v, cache_lf, state_pool, state_shift,
                  state_wkv, page_table):
    nb = x.shape[0]
    x2 = x.reshape(nb, D_MODEL)
    u, q, k, v, _, _, pc, lf = _in_proj(x2, lw["g_mix_pre"], lw["wm"], lw["wf"], lw["bf"], F32)
    ya = _pool_sample(u, jnp.transpose(state_pool, (1, 0, 2)), lw["wp"], lw["pool_scale"])
    lfn = jnp.broadcast_to(lf[:, 0:NH, None], (nb, NH, PAGE_SIZE))
    bias = _decode_bias(page_table, cache_lf, lfn, *consts["bias_mats"])
    yb = _decode_attn(page_table, q.reshape(nb, 1, WIDTH), k.reshape(nb, 1, WIDTH),
                      v.reshape(nb, 1, WIDTH), bias, consts["eye2"], cache_k, cache_v, l)
    rw = lw["rw"] + (consts["seg"],)
    r, w, k2, vv, kn, bv, g, bonus = _rwkv_prep_sample(pc, state_shift, rw)
    e3 = lambda a: a.reshape(1, nb, WIDTH)
    y, s_fin = _scan(e3(r), e3(w), e3(k2), e3(vv), e3(kn), e3(bv), _state_to_tiles(state_wkv),
                     consts["seg2"], consts["eye2"], tc=1)
    yc = _rwkv_post(y.reshape(nb, WIDTH), bonus, g, lw["lnw"], lw["lnb"], consts["seg"], F32)
    x2 = _merge(x2, ya, yb.reshape(nb, WIDTH), yc, lw["g_mix_pre"], lw["g_mix_post"], lw["wg"],
                lw["wa"], lw["wb"], lw["wc"], lw["wo"])
    x2 = _mlp(x2, lw["g_mlp_pre"], lw["g_mlp_post"], lw["wu"], lw["wd"])
    new = (k.reshape(nb, 1, NH, HD), v.reshape(nb, 1, NH, HD), lf[:, 0:NH].reshape(nb, 1, NH),
           jnp.concatenate([state_pool[:, 1:], u[:, None, :]], axis=1), pc,
           _tiles_to_state(s_fin))
    return x2.reshape(nb, 1, D_MODEL), new


def kernel(x_prompt, x_sample, cache_k, cache_v, cache_logf, state_pool, state_shift, state_wkv, page_table, norm_mix_pre, norm_mix_post, norm_mlp_pre, norm_mlp_post, w_in, b_forget, w_pool, pool_scale, rwkv_mu, rwkv_w0, rwkv_w_up, rwkv_a0, rwkv_a_up, rwkv_g_up, rwkv_k_k, rwkv_k_a, rwkv_r_k, rwkv_ln_w, rwkv_ln_b, w_branch_a, w_branch_b, w_branch_c, w_out, w_mlp_up, w_mlp_down):
    p = dict(norm_mix_pre=norm_mix_pre, norm_mix_post=norm_mix_post, norm_mlp_pre=norm_mlp_pre,
             norm_mlp_post=norm_mlp_post, w_in=w_in, b_forget=b_forget, w_pool=w_pool,
             pool_scale=pool_scale, rwkv_mu=rwkv_mu, rwkv_w0=rwkv_w0, rwkv_w_up=rwkv_w_up,
             rwkv_a0=rwkv_a0, rwkv_a_up=rwkv_a_up, rwkv_g_up=rwkv_g_up, rwkv_k_k=rwkv_k_k,
             rwkv_k_a=rwkv_k_a, rwkv_r_k=rwkv_r_k, rwkv_ln_w=rwkv_ln_w, rwkv_ln_b=rwkv_ln_b,
             w_branch_a=w_branch_a, w_branch_b=w_branch_b, w_branch_c=w_branch_c, w_out=w_out,
             w_mlp_up=w_mlp_up, w_mlp_down=w_mlp_down)
    depth, pool_pages = cache_k.shape[0], cache_k.shape[1]
    npages = page_table.shape[1]
    i = jnp.arange(CUM_BLK)
    eye = (jnp.arange(HD)[:, None] == (jnp.arange(LANES) % HD)[None, :]).astype(F32)
    consts = dict(seg=_seg_ones(WIDTH), seg2=_seg_ones(LANES), eye2=eye,
                  tri=(i[:, None] >= i[None, :]).astype(BF16),
                  bias_mats=_decode_bias_mats(npages), bias_lanes=_bias_lane_mats())
    ck = jnp.transpose(cache_k, (0, 1, 3, 4, 2))
    cv = jnp.transpose(cache_v, (0, 1, 3, 4, 2))
    clf = jnp.transpose(cache_logf, (0, 1, 3, 2))
    yp, ys = x_prompt, x_sample
    outs_p, outs_s = [], []
    for l in range(depth):
        lw = _layer_weights(l, p)
        yp, new_p = _prompt_layer(yp, lw, consts)
        outs_p.append(new_p)
        ys, new_s = _sample_layer(ys, lw, consts, l, ck, cv, clf[l], state_pool[l],
                                  state_shift[l], state_wkv[l], page_table)
        outs_s.append(new_s)
    stack = lambda outs: tuple(jnp.stack([o[j] for o in outs]) for j in range(6))
    return (yp, ys) + stack(outs_p) + stack(outs_s)
```

```python
import functools

import jax
import jax.numpy as jnp
from jax import lax
from jax.experimental import pallas as pl
from jax.experimental.pallas import tpu as pltpu

F32 = jnp.float32
BF16 = jnp.bfloat16

D_MODEL = 1024
DEPTH = 4
PAGE_SIZE = 128
POOL_WINDOWS = (2, 4, 8, 16)
A_WIDTH = 512
GROUP_W = 128
POOL_PREFIX = 15
HD = 64
NH = 8
WIDTH = 512
NHP = NH // 2
LANES = 128
NEG_INF = -1e30
DECAY_LORA = 64
AAA_LORA = 64
GATE_LORA = 128
C_IN = 3 * WIDTH + DECAY_LORA + AAA_LORA + GATE_LORA
LN_X_EPS = 64e-5
D_FF = 4 * D_MODEL
RMS_EPS = 1e-6
OFF_F = 4 * WIDTH
OFF_C = OFF_F + NH
OFF_G = OFF_C + C_IN
ATT_SCALE = HD ** -0.5
LOG2E = 1.4426950408889634
VMEM_LIMIT = 56 * 1024 * 1024


def _dot(a, b):
    return jnp.dot(a, b, preferred_element_type=F32)


def _dot_nt(a, b):
    return lax.dot_general(a, b, (((1,), (1,)), ((), ())), preferred_element_type=F32)


def _split2(x):
    hi = x.astype(BF16)
    lo = (x - hi.astype(F32)).astype(BF16)
    return hi, lo


def _dot_left2(x, m):
    hi, lo = _split2(x)
    if m.shape[0] == x.shape[1]:
        return _dot(hi, m) + _dot(lo, m)
    tiles = [slice(i, i + m.shape[0]) for i in range(0, x.shape[1], m.shape[0])]
    return jnp.concatenate([_dot(hi[:, s], m) + _dot(lo[:, s], m) for s in tiles], axis=1)


def _split3(x):
    hi = x.astype(BF16)
    r1 = x - hi.astype(F32)
    mid = r1.astype(BF16)
    lo = (r1 - mid.astype(F32)).astype(BF16)
    return hi, mid, lo


def _dot_left3(x, m):
    hi, mid, lo = _split3(x)
    return _dot(hi, m) + _dot(mid, m) + _dot(lo, m)


def _dot_right3(m, x):
    hi, mid, lo = _split3(x)
    return _dot(m, hi) + _dot(m, mid) + _dot(m, lo)


def _rms(x, g):
    ms = jnp.mean(x * x, axis=-1, keepdims=True)
    return x * lax.rsqrt(ms + RMS_EPS) * g


def _sigmoid(x):
    return 1.0 / (1.0 + jnp.exp(-x))


def _softplus(x):
    return jnp.maximum(x, 0.0) + jnp.log1p(jnp.exp(-jnp.abs(x)))


def _params(sem):
    return pltpu.CompilerParams(dimension_semantics=sem, vmem_limit_bytes=VMEM_LIMIT)


def _const_spec(shape):
    nd = len(shape)
    return pl.BlockSpec(shape, lambda *_: (0,) * nd, pipeline_mode=pl.Buffered(1))


def _row_tile(n, pref):
    return pref if n % pref == 0 else n


def _in_proj_kernel(x_ref, g_ref, wm_ref, wf_ref, bf_ref,
                    u_ref, q_ref, k_ref, v_ref, kb_ref, vb_ref, pc_ref, lf_ref):
    h = _rms(x_ref[...], g_ref[...]).astype(BF16)
    u_ref[...] = _dot(h, wm_ref[:, 0:WIDTH])
    q_ref[...] = (_dot(h, wm_ref[:, WIDTH:2 * WIDTH]) * ATT_SCALE).astype(q_ref.dtype)
    k = _dot(h, wm_ref[:, 2 * WIDTH:3 * WIDTH])
    k_ref[...] = k
    kb_ref[...] = k.astype(BF16)
    v = _dot(h, wm_ref[:, 3 * WIDTH:4 * WIDTH])
    v_ref[...] = v
    vb_ref[...] = v.astype(BF16)
    pc_ref[...] = _dot(h, wm_ref[:, 4 * WIDTH:4 * WIDTH + C_IN])
    f = _dot(h, wf_ref[...]) + bf_ref[...]
    lf_ref[...] = -_softplus(-f)


IN_TT = 64
SCAN_BB = 8
N_RW_WEIGHTS = 9


def _in_proj_prompt_kernel(x_ref, g_ref, wm_ref, wf_ref, bf_ref, perm_ref, *refs, n_alias):
    rw_refs = refs[0:N_RW_WEIGHTS]
    refs = refs[N_RW_WEIGHTS + n_alias:]
    u_ref, q_ref, k_ref, v_ref, kb_ref, vb_ref, lf_ref, pclast_ref = refs[0:8]
    prep_refs = refs[8:-1]
    last_scr = refs[-1]
    nb, tt, _ = x_ref.shape
    rows = nb * tt

    @pl.when(pl.program_id(0) == 0)
    def _():
        last_scr[...] = jnp.zeros(last_scr.shape, F32)

    h = _rms(x_ref[...].reshape(rows, D_MODEL), g_ref[...]).astype(BF16)
    put = lambda ref, val: ref.__setitem__(Ellipsis, val.reshape(nb, tt, val.shape[-1]))
    h_tm = _dot(perm_ref[...], h).astype(BF16)
    pc = _dot(h_tm, wm_ref[:, 4 * WIDTH:4 * WIDTH + C_IN])
    prev = jnp.concatenate([last_scr[...], pc[0:rows - nb]], axis=0)
    last_scr[...] = pc[rows - nb:rows]
    pclast_ref[...] = pc[rows - nb:rows]
    for o_ref, val in zip(prep_refs, _rwkv_prep_math(pc, prev, *rw_refs)):
        o_ref[...] = val.reshape(tt, nb, WIDTH)
    put(u_ref, _dot(h, wm_ref[:, 0:WIDTH]))
    put(q_ref, (_dot(h, wm_ref[:, WIDTH:2 * WIDTH]) * (ATT_SCALE * LOG2E)).astype(q_ref.dtype))
    k = _dot(h, wm_ref[:, 2 * WIDTH:3 * WIDTH])
    put(k_ref, k)
    put(kb_ref, k.astype(BF16))
    v = _dot(h, wm_ref[:, 3 * WIDTH:4 * WIDTH])
    put(v_ref, v)
    put(vb_ref, v.astype(BF16))
    put(lf_ref, -_softplus(-(_dot(h, wf_ref[...]) + bf_ref[...])))


def _in_proj_prompt(x, g, wm, wf, bf, rw, layer, depth, kv_all):
    nb, t, _ = x.shape
    assert nb == SCAN_BB and len(rw) == N_RW_WEIGHTS
    tt = _row_tile(t, IN_TT)
    r = jnp.arange(nb * tt)
    perm = ((r[:, None] % nb) * tt + r[:, None] // nb == r[None, :]).astype(BF16)
    bm = lambda w: pl.BlockSpec((nb, tt, w), lambda i: (0, i, 0))
    tm = pl.BlockSpec((tt, nb, WIDTH), lambda i: (i, 0, 0))
    stacked = pl.BlockSpec((None, nb, tt, WIDTH), lambda i: (layer, 0, i, 0))
    sds = lambda w, dt: jax.ShapeDtypeStruct((nb, t, w), dt)
    all_sds = jax.ShapeDtypeStruct((depth, nb, t, WIDTH), F32)
    n_const = 6 + len(rw)
    return pl.pallas_call(
        functools.partial(_in_proj_prompt_kernel, n_alias=len(kv_all)),
        grid=(t // tt,),
        in_specs=[bm(D_MODEL), _const_spec((1, D_MODEL)), _const_spec(wm.shape),
                  _const_spec(wf.shape), _const_spec((1, LANES)), _const_spec(perm.shape)]
                 + [_const_spec(w.shape) for w in rw]
                 + [pl.BlockSpec(memory_space=pl.ANY) for _ in kv_all],
        out_specs=(bm(WIDTH), bm(WIDTH), stacked, stacked, bm(WIDTH), bm(WIDTH), bm(LANES),
                   pl.BlockSpec((nb, C_IN), lambda i: (0, 0))) + (tm,) * 8,
        out_shape=(sds(WIDTH, F32), sds(WIDTH, BF16), all_sds, all_sds,
                   sds(WIDTH, BF16), sds(WIDTH, BF16), sds(LANES, F32),
                   jax.ShapeDtypeStruct((nb, C_IN), F32))
                  + tuple(jax.ShapeDtypeStruct((t, nb, WIDTH), F32) for _ in range(8)),
        input_output_aliases={n_const + j: 2 + j for j in range(len(kv_all))},
        scratch_shapes=[pltpu.VMEM((nb, C_IN), F32)],
        compiler_params=_params(("arbitrary",)),
        name="in_proj_prompt",
    )(x, g, wm, wf, bf, perm, *rw, *kv_all)


def _in_proj(x, g, wm, wf, bf, q_dtype):
    n = x.shape[0]
    tm = _row_tile(n, 512)
    row = lambda w: pl.BlockSpec((tm, w), lambda i: (i, 0))
    out_shape = (
        jax.ShapeDtypeStruct((n, WIDTH), F32),
        jax.ShapeDtypeStruct((n, WIDTH), q_dtype),
        jax.ShapeDtypeStruct((n, WIDTH), F32),
        jax.ShapeDtypeStruct((n, WIDTH), F32),
        jax.ShapeDtypeStruct((n, WIDTH), BF16),
        jax.ShapeDtypeStruct((n, WIDTH), BF16),
        jax.ShapeDtypeStruct((n, C_IN), F32),
        jax.ShapeDtypeStruct((n, LANES), F32),
    )
    return pl.pallas_call(
        _in_proj_kernel,
        grid=(n // tm,),
        in_specs=[row(D_MODEL), _const_spec((1, D_MODEL)), _const_spec(wm.shape),
                  _const_spec(wf.shape), _const_spec((1, LANES))],
        out_specs=(row(WIDTH), row(WIDTH), row(WIDTH), row(WIDTH), row(WIDTH), row(WIDTH),
                   row(C_IN), row(LANES)),
        out_shape=out_shape,
        compiler_params=_params(("parallel",)),
        name="in_proj",
    )(x, g, wm, wf, bf)


CUM_BLK = 256


BIAS_TERMS = 3


def _cumsum_kernel(lf_ref, tri_ref, pq_ref, pk_ref, oq_ref, ok_ref, eq_ref, ek_ref):
    t = lf_ref.shape[1]
    carry = jnp.zeros((1, LANES), F32)
    for i in range(t // CUM_BLK):
        sl = slice(i * CUM_BLK, (i + 1) * CUM_BLK)
        c = _dot_right3(tri_ref[...], lf_ref[0, sl, :]) + carry
        terms = jnp.concatenate(_split3(c * LOG2E), axis=1)
        eq_ref[0, sl, :] = (_dot(terms, pq_ref[...]) + oq_ref[...]).astype(BF16)
        ek_ref[0, sl, :] = (ok_ref[...] - _dot(terms, pk_ref[...])).astype(BF16)
        carry = c[CUM_BLK - 1:CUM_BLK, :]


def _bias_lane_mats():
    src = jnp.arange(BIAS_TERMS * LANES)
    part, head = src // LANES, src % LANES
    dst = jnp.arange(NH * LANES)
    dhead, dlane = dst // LANES, dst % LANES
    other = HD * (1 - dhead % 2)
    slot = dlane - other
    same = head[:, None] == dhead[None, :]
    pq = (same & (slot[None, :] == part[:, None])).astype(BF16)
    pk = (same & (slot[None, :] == BIAS_TERMS + part[:, None])).astype(BF16)
    oq = ((slot >= BIAS_TERMS) & (slot < 2 * BIAS_TERMS)).astype(F32).reshape(1, -1)
    ok = ((slot >= 0) & (slot < BIAS_TERMS)).astype(F32).reshape(1, -1)
    return pq, pk, oq, ok


def _cumsum(lf, tri, mats):
    b, t, _ = lf.shape
    out = pl.BlockSpec((1, t, NH * LANES), lambda i: (i, 0, 0))
    return pl.pallas_call(
        _cumsum_kernel,
        grid=(b,),
        in_specs=[pl.BlockSpec((1, t, LANES), lambda i: (i, 0, 0)), _const_spec(tri.shape)]
                 + [_const_spec(m.shape) for m in mats],
        out_specs=(out, out),
        out_shape=(jax.ShapeDtypeStruct((b, t, NH * LANES), BF16),
                   jax.ShapeDtypeStruct((b, t, NH * LANES), BF16)),
        compiler_params=_params(("parallel",)),
        name="fox_cumsum",
    )(lf, tri, *mats)


def _flash_kernel(q_ref, k_ref, v_ref, eq_ref, ek_ref, o_ref, m_scr, l_scr, acc_scr, *, tq, tk):
    qi = pl.program_id(1)
    ki = pl.program_id(2)

    @pl.when(ki == 0)
    def _():
        m_scr[...] = jnp.full(m_scr.shape, NEG_INF, F32)
        l_scr[...] = jnp.zeros(l_scr.shape, F32)
        acc_scr[...] = jnp.zeros(acc_scr.shape, F32)

    def block(on_diagonal):
        lane = lax.broadcasted_iota(jnp.int32, (1, LANES), 1)
        own = [((lane // HD) == j).astype(F32).astype(BF16) for j in range(2)]
        if on_diagonal:
            causal = (lax.broadcasted_iota(jnp.int32, (tq, tk), 1)
                      <= lax.broadcasted_iota(jnp.int32, (tq, tk), 0))
        for hp in range(NHP):
            ls = slice(hp * LANES, (hp + 1) * LANES)
            q = q_ref[0, :, ls]
            k = k_ref[0, :, ls]
            v = v_ref[0, :, ls]
            pv = []
            alpha = []
            for j in range(2):
                h = 2 * hp + j
                hs = slice(h * LANES, (h + 1) * LANES)
                s = _dot_nt(q * own[j] + eq_ref[0, :, hs], k * own[j] + ek_ref[0, :, hs])
                if on_diagonal:
                    s = jnp.where(causal, s, NEG_INF)
                m_prev = m_scr[h]
                m_new = jnp.maximum(m_prev, jnp.max(s, axis=1, keepdims=True))
                a = jnp.exp2(m_prev - m_new)
                p = jnp.exp2(s - m_new[:, 0:1])
                l_scr[h] = a * l_scr[h] + jnp.sum(p, axis=1, keepdims=True)
                m_scr[h] = m_new
                pv.append(_dot(p.astype(BF16), v))
                alpha.append(a)
            first = (lane // HD) == 0
            acc_scr[hp] = (jnp.where(first, alpha[0], alpha[1]) * acc_scr[hp]
                           + jnp.where(first, pv[0], pv[1]))

    pl.when(ki < qi)(functools.partial(block, False))
    pl.when(ki == qi)(functools.partial(block, True))

    @pl.when(ki == qi)
    def _():
        lane = lax.broadcasted_iota(jnp.int32, (1, LANES), 1)
        for hp in range(NHP):
            l = jnp.where((lane // HD) == 0, l_scr[2 * hp], l_scr[2 * hp + 1])
            o_ref[0, :, hp * LANES:(hp + 1) * LANES] = (acc_scr[hp] / l).astype(o_ref.dtype)


def _flash(q, k, v, eq, ek, *, tq):
    b, t, _ = q.shape
    tq = min(tq, t)
    nq = t // tq
    kv_idx = lambda bi, qi, ki: (bi, jnp.minimum(ki, qi), 0)
    return pl.pallas_call(
        functools.partial(_flash_kernel, tq=tq, tk=tq),
        grid=(b, nq, nq),
        in_specs=[pl.BlockSpec((1, tq, WIDTH), lambda bi, qi, ki: (bi, qi, 0)),
                  pl.BlockSpec((1, tq, WIDTH), kv_idx),
                  pl.BlockSpec((1, tq, WIDTH), kv_idx),
                  pl.BlockSpec((1, tq, NH * LANES), lambda bi, qi, ki: (bi, qi, 0)),
                  pl.BlockSpec((1, tq, NH * LANES), kv_idx)],
        out_specs=pl.BlockSpec((1, tq, WIDTH), lambda bi, qi, ki: (bi, qi, 0)),
        out_shape=jax.ShapeDtypeStruct((b, t, WIDTH), BF16),
        scratch_shapes=[pltpu.VMEM((NH, tq, LANES), F32), pltpu.VMEM((NH, tq, LANES), F32),
                        pltpu.VMEM((NHP, tq, LANES), F32)],
        compiler_params=_params(("parallel", "parallel", "arbitrary")),
        name="fox_flash",
    )(q, k, v, eq, ek)


def _decode_bias_kernel(pt_ref, tbl_ref, lfn_ref, msuf_ref, mtot_ref, plater_ref, o_ref, g_scr):
    b = pl.program_id(0)
    npages = g_scr.shape[0]
    for p in range(npages):
        g_scr[p] = tbl_ref[pt_ref[b, p]]
    g = g_scr[...].reshape(npages * NH, PAGE_SIZE)
    within = _dot_left3(g, msuf_ref[...])
    tot = _dot_left3(g, mtot_ref[...])
    later = _dot_right3(plater_ref[...], tot)
    o_ref[0] = (within + later).reshape(npages, NH, PAGE_SIZE) + lfn_ref[...]


def _decode_bias(page_table, tbl, lfn, msuf, mtot, plater):
    nb, npages = page_table.shape
    grid_spec = pltpu.PrefetchScalarGridSpec(
        num_scalar_prefetch=1,
        grid=(nb,),
        in_specs=[_const_spec(tbl.shape),
                  pl.BlockSpec((1, NH, PAGE_SIZE), lambda b, pt: (b, 0, 0)),
                  _const_spec(msuf.shape), _const_spec(mtot.shape), _const_spec(plater.shape)],
        out_specs=pl.BlockSpec((1, npages, NH, PAGE_SIZE), lambda b, pt: (b, 0, 0, 0)),
        scratch_shapes=[pltpu.VMEM((npages, NH, PAGE_SIZE), F32)],
    )
    return pl.pallas_call(
        _decode_bias_kernel,
        grid_spec=grid_spec,
        out_shape=jax.ShapeDtypeStruct((nb, npages, NH, PAGE_SIZE), F32),
        compiler_params=_params(("arbitrary",)),
        name="fox_decode_bias",
    )(page_table, tbl, lfn, msuf, mtot, plater)


PAGES_PER_STEP = 16


def _col_bcast(row_pair, eye2, j):
    lane = lax.broadcasted_iota(jnp.int32, (1, LANES), 1)
    diag = jnp.where((lane // HD) == j, jnp.broadcast_to(row_pair, (HD, LANES)) * eye2, 0.0)
    return jnp.broadcast_to(jnp.sum(diag, axis=1, keepdims=True), (HD, LANES))


def _decode_attn_kernel(pt_ref, q_ref, kn_ref, vn_ref, bias_ref, eye_ref, *refs):
    k_refs = refs[0:PAGES_PER_STEP]
    v_refs = refs[PAGES_PER_STEP:2 * PAGES_PER_STEP]
    o_ref, m_scr, l_scr, acc_scr, qc_scr = refs[2 * PAGES_PER_STEP:]
    g = pl.program_id(1)
    eye2 = eye_ref[...]
    lane = lax.broadcasted_iota(jnp.int32, (1, LANES), 1)
    sub = lax.broadcasted_iota(jnp.int32, (NH, PAGE_SIZE), 0)

    @pl.when(g == 0)
    def _():
        q = q_ref[0]
        own = (lax.broadcasted_iota(jnp.int32, (NH, WIDTH), 1) // HD
               == lax.broadcasted_iota(jnp.int32, (NH, WIDTH), 0))
        qk = jnp.broadcast_to(q * kn_ref[0], (NH, WIDTH))
        s_new = jnp.sum(jnp.where(own, qk, 0.0), axis=1, keepdims=True)
        m_scr[...] = jnp.broadcast_to(s_new, m_scr.shape)
        l_scr[...] = jnp.ones(l_scr.shape, F32)
        vn = vn_ref[0]
        for h in range(NH):
            ls = slice((h // 2) * LANES, (h // 2 + 1) * LANES)
            qc_scr[h] = _col_bcast(q[:, ls], eye2, h % 2)
            acc_scr[h] = jnp.where(lane == 0, _col_bcast(vn[:, ls], eye2, h % 2), 0.0)

    s_parts = []
    for j in range(PAGES_PER_STEP):
        s_page = jnp.zeros((NH, PAGE_SIZE), F32)
        for h in range(NH):
            sh = jnp.sum(k_refs[j][0, 0, h] * qc_scr[h], axis=0, keepdims=True)
            s_page = jnp.where(sub == h, jnp.broadcast_to(sh, (NH, PAGE_SIZE)), s_page)
        s_parts.append(s_page + bias_ref[0, j])
    s = jnp.concatenate(s_parts, axis=1)
    m_prev = m_scr[...]
    m_new = jnp.maximum(m_prev, jnp.max(s, axis=1, keepdims=True))
    a = jnp.exp(m_prev - m_new)
    p = jnp.exp(s - m_new[:, 0:1])
    l_scr[...] = a * l_scr[...] + jnp.sum(p, axis=1, keepdims=True)
    m_scr[...] = m_new
    for h in range(NH):
        acc = acc_scr[h] * a[h:h + 1, 0:1]
        for j in range(PAGES_PER_STEP):
            ph = p[h:h + 1, j * PAGE_SIZE:(j + 1) * PAGE_SIZE]
            acc = acc + v_refs[j][0, 0, h] * jnp.broadcast_to(ph, (HD, PAGE_SIZE))
        acc_scr[h] = acc

    @pl.when(g == pl.num_programs(1) - 1)
    def _():
        rows = []
        for hp in range(NHP):
            pair = []
            for j in range(2):
                h = 2 * hp + j
                col = jnp.sum(acc_scr[h], axis=1, keepdims=True) / l_scr[h:h + 1, 0:1]
                pair.append(jnp.sum(jnp.broadcast_to(col, (HD, LANES)) * eye2, axis=0,
                                    keepdims=True))
            rows.append(jnp.where((lane // HD) == 0, pair[0], pair[1]))
        o_ref[0] = jnp.concatenate(rows, axis=1)


def _decode_attn(page_table, q, k_new, v_new, bias, eye2, cache_kt, cache_vt, layer):
    nb, npages = page_table.shape
    ng = npages // PAGES_PER_STEP

    def page_spec(j):
        return pl.BlockSpec((1, 1, NH, HD, PAGE_SIZE),
                            lambda b, g, pt: (layer, pt[b, g * PAGES_PER_STEP + j], 0, 0, 0))

    vec = lambda: pl.BlockSpec((1, 1, WIDTH), lambda b, g, pt: (b, 0, 0))
    grid_spec = pltpu.PrefetchScalarGridSpec(
        num_scalar_prefetch=1,
        grid=(nb, ng),
        in_specs=[vec(), vec(), vec(),
                  pl.BlockSpec((1, PAGES_PER_STEP, NH, PAGE_SIZE), lambda b, g, pt: (b, g, 0, 0)),
                  _const_spec(eye2.shape)]
                 + [page_spec(j) for j in range(PAGES_PER_STEP)]
                 + [page_spec(j) for j in range(PAGES_PER_STEP)],
        out_specs=pl.BlockSpec((1, 1, WIDTH), lambda b, g, pt: (b, 0, 0)),
        scratch_shapes=[pltpu.VMEM((NH, LANES), F32), pltpu.VMEM((NH, LANES), F32),
                        pltpu.VMEM((NH, HD, PAGE_SIZE), F32), pltpu.VMEM((NH, HD, LANES), F32)],
    )
    return pl.pallas_call(
        _decode_attn_kernel,
        grid_spec=grid_spec,
        out_shape=jax.ShapeDtypeStruct((nb, 1, WIDTH), F32),
        compiler_params=_params(("parallel", "arbitrary")),
        name="fox_decode_attn",
    )(page_table, q, k_new, v_new, bias, eye2,
      *([cache_kt] * PAGES_PER_STEP), *([cache_vt] * PAGES_PER_STEP))


POOL_HALO = 16


def _pool_project(pooled, wp_ref, sc_ref):
    outs = []
    for g in range(len(POOL_WINDOWS)):
        ls = slice(g * GROUP_W, (g + 1) * GROUP_W)
        outs.append(_dot(pooled[g].astype(BF16), wp_ref[g]) * sc_ref[:, ls])
    return outs


def _pool_prompt_kernel(u_ref, wp_ref, sc_ref, o_ref, ext_scr, *, tt):
    ti = pl.program_id(1)

    @pl.when(ti == 0)
    def _():
        ext_scr[0:POOL_HALO, :] = jnp.zeros((POOL_HALO, A_WIDTH), F32)

    ext_scr[POOL_HALO:POOL_HALO + tt, :] = u_ref[0]
    pos1 = ti * tt + lax.broadcasted_iota(jnp.int32, (tt, GROUP_W), 0) + 1
    pooled = []
    for g, w in enumerate(POOL_WINDOWS):
        ls = slice(g * GROUP_W, (g + 1) * GROUP_W)
        x = ext_scr[POOL_HALO:POOL_HALO + tt, ls]
        win = x
        for d in range(1, w):
            win = win + ext_scr[POOL_HALO - d:POOL_HALO - d + tt, ls]
        cnt = jnp.minimum(w, pos1).astype(F32)
        pooled.append(win / cnt - x)
    outs = _pool_project(pooled, wp_ref, sc_ref)
    for g in range(len(POOL_WINDOWS)):
        o_ref[0, :, g * GROUP_W:(g + 1) * GROUP_W] = outs[g].astype(o_ref.dtype)
    ext_scr[0:POOL_HALO, :] = ext_scr[tt:tt + POOL_HALO, :]


def _pool_prompt(u, wp, sc):
    b, t, _ = u.shape
    tt = _row_tile(t, 512)
    return pl.pallas_call(
        functools.partial(_pool_prompt_kernel, tt=tt),
        grid=(b, t // tt),
        in_specs=[pl.BlockSpec((1, tt, A_WIDTH), lambda bi, ti: (bi, ti, 0)),
                  _const_spec(wp.shape), _const_spec((1, A_WIDTH))],
        out_specs=pl.BlockSpec((1, tt, A_WIDTH), lambda bi, ti: (bi, ti, 0)),
        out_shape=jax.ShapeDtypeStruct((b, t, A_WIDTH), BF16),
        scratch_shapes=[pltpu.VMEM((POOL_HALO + tt, A_WIDTH), F32)],
        compiler_params=_params(("parallel", "arbitrary")),
        name="pool_prompt",
    )(u, wp, sc)


def _pool_sample_kernel(u_ref, prev_ref, wp_ref, sc_ref, o_ref):
    pooled = []
    for g, w in enumerate(POOL_WINDOWS):
        ls = slice(g * GROUP_W, (g + 1) * GROUP_W)
        x = u_ref[:, ls]
        win = x
        for d in range(1, w):
            win = win + prev_ref[POOL_PREFIX - d, :, ls]
        pooled.append(win / float(w) - x)
    outs = _pool_project(pooled, wp_ref, sc_ref)
    for g in range(len(POOL_WINDOWS)):
        o_ref[:, g * GROUP_W:(g + 1) * GROUP_W] = outs[g]


def _pool_sample(u, prev_t, wp, sc):
    nb = u.shape[0]
    return pl.pallas_call(
        _pool_sample_kernel,
        out_shape=jax.ShapeDtypeStruct((nb, A_WIDTH), F32),
        compiler_params=pltpu.CompilerParams(vmem_limit_bytes=VMEM_LIMIT),
        name="pool_sample",
    )(u, prev_t, wp, sc)


def _rwkv_prep_math(pc, prev, mu_ref, wlo_ref, w0_ref, a0_ref, gup_ref, kk_ref, ka_ref, rk_ref,
                    seg_ref):
    xs = pc + (prev - pc) * mu_ref[...]
    r = xs[:, 0:WIDTH]
    k = xs[:, WIDTH:2 * WIDTH]
    v = xs[:, 2 * WIDTH:3 * WIDTH]
    lora = xs[:, 3 * WIDTH:3 * WIDTH + LANES]
    gd = xs[:, 3 * WIDTH + LANES:3 * WIDTH + 2 * LANES]
    lane = lax.broadcasted_iota(jnp.int32, (1, LANES), 1)
    lhs = jnp.where(lane < DECAY_LORA, jnp.tanh(lora), lora).astype(BF16)
    up = _dot(lhs, wlo_ref[...])
    w = -_softplus(-(w0_ref[...] + up[:, 0:WIDTH])) - 0.5
    decay = jnp.exp(-jnp.exp(w))
    a = _sigmoid(a0_ref[...] + up[:, WIDTH:2 * WIDTH])
    g = _dot(_sigmoid(gd).astype(BF16), gup_ref[...])
    kk = k * kk_ref[...]
    nrm = jnp.sqrt(_dot_left2(kk * kk, seg_ref[...]))
    kk = kk / jnp.maximum(nrm, 1e-12)
    k2 = k * (1.0 + (a - 1.0) * ka_ref[...])
    bonus = _dot_left2(r * k2 * rk_ref[...], seg_ref[...]) * v
    return r, decay, k2, v, -kk, kk * a, g, bonus


def _rwkv_prep_sample_kernel(pc_ref, prev_ref, mu_ref, wlo_ref, w0_ref, a0_ref, gup_ref, kk_ref,
                             ka_ref, rk_ref, seg_ref, *out_refs):
    outs = _rwkv_prep_math(pc_ref[...], prev_ref[...], mu_ref, wlo_ref, w0_ref, a0_ref, gup_ref,
                           kk_ref, ka_ref, rk_ref, seg_ref)
    for o_ref, val in zip(out_refs, outs):
        o_ref[...] = val


def _rwkv_prep_sample(pc, prev, weights):
    nb = pc.shape[0]
    return pl.pallas_call(
        _rwkv_prep_sample_kernel,
        out_shape=tuple(jax.ShapeDtypeStruct((nb, WIDTH), F32) for _ in range(8)),
        compiler_params=pltpu.CompilerParams(vmem_limit_bytes=VMEM_LIMIT),
        name="rwkv_prep_sample",
    )(pc, prev, *weights)


def _scan_kernel(r_ref, w_ref, k_ref, v_ref, kn_ref, b_ref, s0_ref, seg_ref, eye_ref,
                 y_ref, so_ref, s_scr, *, tc, zero_init):
    ci = pl.program_id(1)

    @pl.when(ci == 0)
    def _():
        if zero_init:
            s_scr[...] = jnp.zeros(s_scr.shape, F32)
        else:
            s_scr[...] = s0_ref[...]

    seg = seg_ref[...]
    eye2 = eye_ref[...]
    groups = [(b, hp) for b in range(SCAN_BB) for hp in range(NHP)]
    ng = len(groups)
    sub = lax.broadcasted_iota(jnp.int32, (SCAN_BB, LANES), 0)

    def row(slab, b, hp):
        return jnp.broadcast_to(slab[b:b + 1, hp * LANES:(hp + 1) * LANES], (HD, LANES))

    def row16(slab, b, hp):
        return row(slab, b, hp).astype(BF16)

    def seg_sum(parts):
        return _dot(jnp.concatenate(parts, axis=0), seg)

    eye16 = eye2.astype(BF16)

    def write_y(yb, t):
        y_tiles = [jnp.zeros((SCAN_BB, LANES), F32) for _ in range(NHP)]
        for n, (b, hp) in enumerate(groups):
            y_row = jnp.sum(yb[n * HD:(n + 1) * HD] * eye2, axis=0, keepdims=True)
            y_tiles[hp] = jnp.where(sub == b, jnp.broadcast_to(y_row, (SCAN_BB, LANES)),
                                    y_tiles[hp])
        y_ref[t] = jnp.concatenate(y_tiles, axis=1)

    def step(t, carry):
        tp = jnp.maximum(t - 1, 0)
        w_t, k_t, v_t, kn_t, b_t = (ref[t] for ref in (w_ref, k_ref, v_ref, kn_ref, b_ref))
        r_p = r_ref[tp]
        s16 = [s_scr[b, hp].astype(BF16) for b, hp in groups]
        res = seg_sum([s16[n] * row16(kn_t, b, hp) for n, (b, hp) in enumerate(groups)]
                      + [eye16 * row16(v_t, b, hp) for b, hp in groups]
                      + [s16[n] * row16(r_p, b, hp) for n, (b, hp) in enumerate(groups)])
        for n, (b, hp) in enumerate(groups):
            sa = res[n * HD:(n + 1) * HD]
            vb = res[(ng + n) * HD:(ng + n + 1) * HD]
            s_scr[b, hp] = (s_scr[b, hp] * row(w_t, b, hp) + sa * row(b_t, b, hp)
                            + vb * row(k_t, b, hp))
        write_y(res[2 * ng * HD:], tp)
        return carry

    lax.fori_loop(0, tc, step, 0)
    r_last = r_ref[tc - 1]
    write_y(seg_sum([s_scr[b, hp].astype(BF16) * row16(r_last, b, hp) for b, hp in groups]),
            tc - 1)

    @pl.when(ci == pl.num_programs(1) - 1)
    def _():
        so_ref[...] = s_scr[...]


def _scan(r, w, k, v, kn, bv, s0, seg2, eye2, *, tc):
    t, nb, _ = r.shape
    tc = min(tc, t)
    zero_init = s0 is None
    if zero_init:
        s0 = jnp.zeros((SCAN_BB, NHP, HD, LANES), F32)
        s0_spec = pl.BlockSpec((SCAN_BB, NHP, HD, LANES), lambda bi, ci: (0, 0, 0, 0))
    else:
        s0_spec = pl.BlockSpec((SCAN_BB, NHP, HD, LANES), lambda bi, ci: (bi, 0, 0, 0))
    seq = lambda: pl.BlockSpec((tc, SCAN_BB, WIDTH), lambda bi, ci: (ci, bi, 0))
    return pl.pallas_call(
        functools.partial(_scan_kernel, tc=tc, zero_init=zero_init),
        grid=(nb // SCAN_BB, t // tc),
        in_specs=[seq(), seq(), seq(), seq(), seq(), seq(), s0_spec,
                  _const_spec(seg2.shape), _const_spec(eye2.shape)],
        out_specs=(seq(), pl.BlockSpec((SCAN_BB, NHP, HD, LANES), lambda bi, ci: (bi, 0, 0, 0))),
        out_shape=(jax.ShapeDtypeStruct((t, nb, WIDTH), F32),
                   jax.ShapeDtypeStruct((nb, NHP, HD, LANES), F32)),
        scratch_shapes=[pltpu.VMEM((SCAN_BB, NHP, HD, LANES), F32)],
        compiler_params=_params(("parallel", "arbitrary")),
        name="rwkv_scan",
    )(r, w, k, v, kn, bv, s0, seg2, eye2)


def _rwkv_post_kernel(y_ref, bonus_ref, g_ref, lnw_ref, lnb_ref, seg_ref, o_ref):
    y = y_ref[...]
    seg = seg_ref[...]
    mean = _dot_left2(y, seg) * (1.0 / HD)
    d = y - mean
    var = _dot_left2(d * d, seg) * (1.0 / HD)
    yn = d * lax.rsqrt(var + LN_X_EPS) * lnw_ref[...] + lnb_ref[...]
    o_ref[...] = ((yn + bonus_ref[...]) * g_ref[...]).astype(o_ref.dtype)


def _rwkv_post(y, bonus, g, lnw, lnb, seg, out_dtype):
    n = y.shape[0]
    tm = _row_tile(n, 512)
    row = pl.BlockSpec((tm, WIDTH), lambda i: (i, 0))
    return pl.pallas_call(
        _rwkv_post_kernel,
        grid=(n // tm,),
        in_specs=[row, row, row, _const_spec((1, WIDTH)), _const_spec((1, WIDTH)),
                  _const_spec(seg.shape)],
        out_specs=row,
        out_shape=jax.ShapeDtypeStruct((n, WIDTH), out_dtype),
        compiler_params=_params(("parallel",)),
        name="rwkv_post",
    )(y, bonus, g, lnw, lnb, seg)


def _merge_kernel(x_ref, ya_ref, yb_ref, yc_ref, gpre_ref, gpost_ref, wg_ref, wa_ref, wb_ref,
                  wc_ref, wo_ref, o_ref):
    x = x_ref[...]
    h = _rms(x, gpre_ref[...]).astype(BF16)
    merged = None
    for n, (y_ref, w_ref) in enumerate(((ya_ref, wa_ref), (yb_ref, wb_ref), (yc_ref, wc_ref))):
        gate = _sigmoid(_dot(h, wg_ref[:, n * D_MODEL:(n + 1) * D_MODEL]))
        term = gate * _dot(y_ref[...].astype(BF16), w_ref[...])
        merged = term if merged is None else merged + term
    o = _dot(merged.astype(BF16), wo_ref[...])
    o_ref[...] = x + _rms(o, gpost_ref[...])


def _merge(x, ya, yb, yc, gpre, gpost, wg, wa, wb, wc, wo):
    n = x.shape[0]
    tm = _row_tile(n, 512)
    row = lambda w: pl.BlockSpec((tm, w), lambda i: (i, 0))
    return pl.pallas_call(
        _merge_kernel,
        grid=(n // tm,),
        in_specs=[row(D_MODEL), row(WIDTH), row(WIDTH), row(WIDTH),
                  _const_spec((1, D_MODEL)), _const_spec((1, D_MODEL)),
                  _const_spec(wg.shape), _const_spec(wa.shape), _const_spec(wb.shape),
                  _const_spec(wc.shape), _const_spec(wo.shape)],
        out_specs=row(D_MODEL),
        out_shape=jax.ShapeDtypeStruct((n, D_MODEL), F32),
        compiler_params=_params(("parallel",)),
        name="merge",
    )(x, ya, yb, yc, gpre, gpost, wg, wa, wb, wc, wo)


def _mlp_kernel(x_ref, gpre_ref, gpost_ref, wu_ref, wd_ref, o_ref):
    x = x_ref[...]
    h = _rms(x, gpre_ref[...]).astype(BF16)
    up = jnp.maximum(_dot(h, wu_ref[...]), 0.0)
    down = _dot((up * up).astype(BF16), wd_ref[...])
    o_ref[...] = x + _rms(down, gpost_ref[...])


def _mlp(x, gpre, gpost, wu, wd):
    n = x.shape[0]
    tm = _row_tile(n, 512)
    row = pl.BlockSpec((tm, D_MODEL), lambda i: (i, 0))
    return pl.pallas_call(
        _mlp_kernel,
        grid=(n // tm,),
        in_specs=[row, _const_spec((1, D_MODEL)), _const_spec((1, D_MODEL)),
                  _const_spec(wu.shape), _const_spec(wd.shape)],
        out_specs=row,
        out_shape=jax.ShapeDtypeStruct((n, D_MODEL), F32),
        compiler_params=_params(("parallel",)),
        name="mlp",
    )(x, gpre, gpost, wu, wd)


def _seg_ones(n):
    i = jnp.arange(n)
    return (i[:, None] // HD == i[None, :] // HD).astype(BF16)


def _decode_bias_mats(npages):
    pos = jnp.arange(PAGE_SIZE)
    msuf = (pos[:, None] > pos[None, :]).astype(BF16)
    mtot = jnp.ones((PAGE_SIZE, PAGE_SIZE), BF16)
    r = jnp.arange(npages * NH)
    plater = ((r[None, :] // NH > r[:, None] // NH)
              & (r[None, :] % NH == r[:, None] % NH)).astype(BF16)
    return msuf, mtot, plater


def _state_to_tiles(s):
    nb = s.shape[0]
    return s.reshape(nb, NHP, 2, HD, HD).transpose(0, 1, 3, 2, 4).reshape(nb, NHP, HD, LANES)


def _tiles_to_state(s):
    nb = s.shape[0]
    return s.reshape(nb, NHP, HD, 2, HD).transpose(0, 1, 3, 2, 4).reshape(nb, NH, HD, HD)


def _layer_weights(l, p):
    w_in = p["w_in"][l]
    row = lambda a: a[l].reshape(1, -1)
    wlo = jnp.zeros((LANES, 2 * WIDTH), F32)
    wlo = wlo.at[0:DECAY_LORA, 0:WIDTH].set(p["rwkv_w_up"][l])
    wlo = wlo.at[DECAY_LORA:, WIDTH:].set(p["rwkv_a_up"][l])
    return dict(
        g_mix_pre=row(p["norm_mix_pre"]), g_mix_post=row(p["norm_mix_post"]),
        g_mlp_pre=row(p["norm_mlp_pre"]), g_mlp_post=row(p["norm_mlp_post"]),
        wm=jnp.concatenate([w_in[:, 0:OFF_F], w_in[:, OFF_C:OFF_G]], axis=1).astype(BF16),
        wf=jnp.pad(w_in[:, OFF_F:OFF_C], ((0, 0), (0, LANES - NH))).astype(BF16),
        bf=jnp.pad(p["b_forget"][l], (0, LANES - NH)).reshape(1, LANES),
        wg=w_in[:, OFF_G:].astype(BF16),
        wp=p["w_pool"][l].astype(BF16), pool_scale=row(p["pool_scale"]),
        rw=(row(p["rwkv_mu"]), wlo.astype(BF16), row(p["rwkv_w0"]), row(p["rwkv_a0"]),
            p["rwkv_g_up"][l].astype(BF16), row(p["rwkv_k_k"]), row(p["rwkv_k_a"]),
            row(p["rwkv_r_k"])),
        lnw=row(p["rwkv_ln_w"]), lnb=row(p["rwkv_ln_b"]),
        wa=p["w_branch_a"][l].astype(BF16), wb=p["w_branch_b"][l].astype(BF16),
        wc=p["w_branch_c"][l].astype(BF16), wo=p["w_out"][l].astype(BF16),
        wu=p["w_mlp_up"][l].astype(BF16), wd=p["w_mlp_down"][l].astype(BF16),
    )


def _prompt_layer(x, lw, consts, layer, depth, kv_all):
    b, t, _ = x.shape
    n = b * t
    x2 = x.reshape(n, D_MODEL)
    (u, q, k_all, v_all, kb, vb, lf, pc_last, r, w, k2, vv, kn, bv, g, bonus) = _in_proj_prompt(
        x, lw["g_mix_pre"], lw["wm"], lw["wf"], lw["bf"], lw["rw"] + (consts["seg2"],),
        layer, depth, kv_all)
    ya = _pool_prompt(u, lw["wp"], lw["pool_scale"])
    eq, ek = _cumsum(lf, consts["tri"], consts["bias_lanes"])
    yb = _flash(q, kb, vb, eq, ek, tq=512)
    y, s_fin = _scan(r, w, k2, vv, kn, bv, None, consts["seg2"], consts["eye2"], tc=128)
    yc = _rwkv_post(y.reshape(n, WIDTH), bonus.reshape(n, WIDTH), g.reshape(n, WIDTH),
                    lw["lnw"], lw["lnb"], consts["seg2"], BF16)
    yc = jnp.transpose(yc.reshape(t, b, WIDTH), (1, 0, 2)).reshape(n, WIDTH)
    x2 = _merge(x2, ya.reshape(n, WIDTH), yb.reshape(n, WIDTH), yc, lw["g_mix_pre"],
                lw["g_mix_post"], lw["wg"], lw["wa"], lw["wb"], lw["wc"], lw["wo"])
    x2 = _mlp(x2, lw["g_mlp_pre"], lw["g_mlp_post"], lw["wu"], lw["wd"])
    new = (lf[:, :, 0:NH], u[:, t - POOL_PREFIX:], pc_last, _tiles_to_state(s_fin))
    return x2.reshape(b, t, D_MODEL), new, (k_all, v_all)


def _sample_layer(x, lw, consts, l, cache_k, cache_v, cache_lf, state_pool, state_shift,
                  state_wkv, page_table):
    nb = x.shape[0]
    x2 = x.reshape(nb, D_MODEL)
    u, q, k, v, _, _, pc, lf = _in_proj(x2, lw["g_mix_pre"], lw["wm"], lw["wf"], lw["bf"], F32)
    ya = _pool_sample(u, jnp.transpose(state_pool, (1, 0, 2)), lw["wp"], lw["pool_scale"])
    lfn = jnp.broadcast_to(lf[:, 0:NH, None], (nb, NH, PAGE_SIZE))
    bias = _decode_bias(page_table, cache_lf, lfn, *consts["bias_mats"])
    yb = _decode_attn(page_table, q.reshape(nb, 1, WIDTH), k.reshape(nb, 1, WIDTH),
                      v.reshape(nb, 1, WIDTH), bias, consts["eye2"], cache_k, cache_v, l)
    rw = lw["rw"] + (consts["seg2"],)
    r, w, k2, vv, kn, bv, g, bonus = _rwkv_prep_sample(pc, state_shift, rw)
    e3 = lambda a: a.reshape(1, nb, WIDTH)
    y, s_fin = _scan(e3(r), e3(w), e3(k2), e3(vv), e3(kn), e3(bv), _state_to_tiles(state_wkv),
                     consts["seg2"], consts["eye2"], tc=1)
    yc = _rwkv_post(y.reshape(nb, WIDTH), bonus, g, lw["lnw"], lw["lnb"], consts["seg2"], F32)
    x2 = _merge(x2, ya, yb.reshape(nb, WIDTH), yc, lw["g_mix_pre"], lw["g_mix_post"], lw["wg"],
                lw["wa"], lw["wb"], lw["wc"], lw["wo"])
    x2 = _mlp(x2, lw["g_mlp_pre"], lw["g_mlp_post"], lw["wu"], lw["wd"])
    new = (k.reshape(nb, 1, NH, HD), v.reshape(nb, 1, NH, HD), lf[:, 0:NH].reshape(nb, 1, NH),
           jnp.concatenate([state_pool[:, 1:], u[:, None, :]], axis=1), pc,
           _tiles_to_state(s_fin))
    return x2.reshape(nb, 1, D_MODEL), new


def kernel(x_prompt, x_sample, cache_k, cache_v, cache_logf, state_pool, state_shift, state_wkv, page_table, norm_mix_pre, norm_mix_post, norm_mlp_pre, norm_mlp_post, w_in, b_forget, w_pool, pool_scale, rwkv_mu, rwkv_w0, rwkv_w_up, rwkv_a0, rwkv_a_up, rwkv_g_up, rwkv_k_k, rwkv_k_a, rwkv_r_k, rwkv_ln_w, rwkv_ln_b, w_branch_a, w_branch_b, w_branch_c, w_out, w_mlp_up, w_mlp_down):
    p = dict(norm_mix_pre=norm_mix_pre, norm_mix_post=norm_mix_post, norm_mlp_pre=norm_mlp_pre,
             norm_mlp_post=norm_mlp_post, w_in=w_in, b_forget=b_forget, w_pool=w_pool,
             pool_scale=pool_scale, rwkv_mu=rwkv_mu, rwkv_w0=rwkv_w0, rwkv_w_up=rwkv_w_up,
             rwkv_a0=rwkv_a0, rwkv_a_up=rwkv_a_up, rwkv_g_up=rwkv_g_up, rwkv_k_k=rwkv_k_k,
             rwkv_k_a=rwkv_k_a, rwkv_r_k=rwkv_r_k, rwkv_ln_w=rwkv_ln_w, rwkv_ln_b=rwkv_ln_b,
             w_branch_a=w_branch_a, w_branch_b=w_branch_b, w_branch_c=w_branch_c, w_out=w_out,
             w_mlp_up=w_mlp_up, w_mlp_down=w_mlp_down)
    depth, pool_pages = cache_k.shape[0], cache_k.shape[1]
    npages = page_table.shape[1]
    i = jnp.arange(CUM_BLK)
    eye = (jnp.arange(HD)[:, None] == (jnp.arange(LANES) % HD)[None, :]).astype(F32)
    consts = dict(seg2=_seg_ones(LANES), eye2=eye,
                  tri=(i[:, None] >= i[None, :]).astype(BF16),
                  bias_mats=_decode_bias_mats(npages), bias_lanes=_bias_lane_mats())
    ck = jnp.transpose(cache_k, (0, 1, 3, 4, 2))
    cv = jnp.transpose(cache_v, (0, 1, 3, 4, 2))
    clf = jnp.transpose(cache_logf, (0, 1, 3, 2))
    yp, ys = x_prompt, x_sample
    outs_p, outs_s = [], []
    kv_all = ()
    for l in range(depth):
        lw = _layer_weights(l, p)
        yp, new_p, kv_all = _prompt_layer(yp, lw, consts, l, depth, kv_all)
        outs_p.append(new_p)
        ys, new_s = _sample_layer(ys, lw, consts, l, ck, cv, clf[l], state_pool[l],
                                  state_shift[l], state_wkv[l], page_table)
        outs_s.append(new_s)
    stack = lambda outs: tuple(jnp.stack([o[j] for o in outs]) for j in range(len(outs[0])))
    bp, tp = x_prompt.shape[0], x_prompt.shape[1]
    p_kv = tuple(a.reshape(depth, bp, tp, NH, HD) for a in kv_all)
    return (yp, ys) + p_kv + stack(outs_p) + stack(outs_s)
```

```python
import functools

import jax
import jax.numpy as jnp
from jax import lax
from jax.experimental import pallas as pl
from jax.experimental.pallas import tpu as pltpu

F32 = jnp.float32
BF16 = jnp.bfloat16

D_MODEL = 1024
DEPTH = 4
PAGE_SIZE = 128
POOL_WINDOWS = (2, 4, 8, 16)
A_WIDTH = 512
GROUP_W = 128
POOL_PREFIX = 15
HD = 64
NH = 8
WIDTH = 512
NHP = NH // 2
LANES = 128
NEG_INF = -1e30
DECAY_LORA = 64
AAA_LORA = 64
GATE_LORA = 128
C_IN = 3 * WIDTH + DECAY_LORA + AAA_LORA + GATE_LORA
LN_X_EPS = 64e-5
D_FF = 4 * D_MODEL
RMS_EPS = 1e-6
OFF_F = 4 * WIDTH
OFF_C = OFF_F + NH
OFF_G = OFF_C + C_IN
ATT_SCALE = HD ** -0.5
LOG2E = 1.4426950408889634
VMEM_LIMIT = 56 * 1024 * 1024


def _dot(a, b):
    return jnp.dot(a, b, preferred_element_type=F32)


def _dot_nt(a, b):
    return lax.dot_general(a, b, (((1,), (1,)), ((), ())), preferred_element_type=F32)


def _split2(x):
    hi = x.astype(BF16)
    lo = (x - hi.astype(F32)).astype(BF16)
    return hi, lo


def _dot_left2(x, m):
    hi, lo = _split2(x)
    if m.shape[0] == x.shape[1]:
        return _dot(hi, m) + _dot(lo, m)
    tiles = [slice(i, i + m.shape[0]) for i in range(0, x.shape[1], m.shape[0])]
    return jnp.concatenate([_dot(hi[:, s], m) + _dot(lo[:, s], m) for s in tiles], axis=1)


def _split3(x):
    hi = x.astype(BF16)
    r1 = x - hi.astype(F32)
    mid = r1.astype(BF16)
    lo = (r1 - mid.astype(F32)).astype(BF16)
    return hi, mid, lo


def _dot_left3(x, m):
    hi, mid, lo = _split3(x)
    return _dot(hi, m) + _dot(mid, m) + _dot(lo, m)


def _dot_right3(m, x):
    hi, mid, lo = _split3(x)
    return _dot(m, hi) + _dot(m, mid) + _dot(m, lo)


def _rms(x, g):
    ms = jnp.mean(x * x, axis=-1, keepdims=True)
    return x * lax.rsqrt(ms + RMS_EPS) * g


def _sigmoid(x):
    return 1.0 / (1.0 + jnp.exp(-x))


def _softplus(x):
    return jnp.maximum(x, 0.0) + jnp.log1p(jnp.exp(-jnp.abs(x)))


def _params(sem):
    return pltpu.CompilerParams(dimension_semantics=sem, vmem_limit_bytes=VMEM_LIMIT)


def _const_spec(shape):
    nd = len(shape)
    return pl.BlockSpec(shape, lambda *_: (0,) * nd, pipeline_mode=pl.Buffered(1))


def _row_tile(n, pref):
    return pref if n % pref == 0 else n


def _in_proj_kernel(x_ref, g_ref, wm_ref, wf_ref, bf_ref,
                    u_ref, q_ref, k_ref, v_ref, kb_ref, vb_ref, pc_ref, lf_ref):
    h = _rms(x_ref[...], g_ref[...]).astype(BF16)
    u_ref[...] = _dot(h, wm_ref[:, 0:WIDTH])
    q_ref[...] = (_dot(h, wm_ref[:, WIDTH:2 * WIDTH]) * ATT_SCALE).astype(q_ref.dtype)
    k = _dot(h, wm_ref[:, 2 * WIDTH:3 * WIDTH])
    k_ref[...] = k
    kb_ref[...] = k.astype(BF16)
    v = _dot(h, wm_ref[:, 3 * WIDTH:4 * WIDTH])
    v_ref[...] = v
    vb_ref[...] = v.astype(BF16)
    pc_ref[...] = _dot(h, wm_ref[:, 4 * WIDTH:4 * WIDTH + C_IN])
    f = _dot(h, wf_ref[...]) + bf_ref[...]
    lf_ref[...] = -_softplus(-f)


IN_TT = 64
SCAN_BB = 8
N_RW_WEIGHTS = 9


def _in_proj_prompt_kernel(x_ref, g_ref, wm_ref, wf_ref, bf_ref, perm_ref, *refs, n_alias):
    rw_refs = refs[0:N_RW_WEIGHTS]
    refs = refs[N_RW_WEIGHTS + n_alias:]
    u_ref, q_ref, k_ref, v_ref, kb_ref, vb_ref, lf_ref, pclast_ref = refs[0:8]
    prep_refs = refs[8:-1]
    last_scr = refs[-1]
    nb, tt, _ = x_ref.shape
    rows = nb * tt

    @pl.when(pl.program_id(0) == 0)
    def _():
        last_scr[...] = jnp.zeros(last_scr.shape, F32)

    h = _rms(x_ref[...].reshape(rows, D_MODEL), g_ref[...]).astype(BF16)
    put = lambda ref, val: ref.__setitem__(Ellipsis, val.reshape(nb, tt, val.shape[-1]))
    h_tm = _dot(perm_ref[...], h).astype(BF16)
    pc = _dot(h_tm, wm_ref[:, 4 * WIDTH:4 * WIDTH + C_IN])
    prev = jnp.concatenate([last_scr[...], pc[0:rows - nb]], axis=0)
    last_scr[...] = pc[rows - nb:rows]
    pclast_ref[...] = pc[rows - nb:rows]
    for o_ref, val in zip(prep_refs, _rwkv_prep_math(pc, prev, *rw_refs)):
        o_ref[...] = val.reshape(tt, nb, WIDTH)
    put(u_ref, _dot(h, wm_ref[:, 0:WIDTH]))
    put(q_ref, (_dot(h, wm_ref[:, WIDTH:2 * WIDTH]) * (ATT_SCALE * LOG2E)).astype(q_ref.dtype))
    k = _dot(h, wm_ref[:, 2 * WIDTH:3 * WIDTH])
    put(k_ref, k)
    put(kb_ref, k.astype(BF16))
    v = _dot(h, wm_ref[:, 3 * WIDTH:4 * WIDTH])
    put(v_ref, v)
    put(vb_ref, v.astype(BF16))
    put(lf_ref, -_softplus(-(_dot(h, wf_ref[...]) + bf_ref[...])))


def _in_proj_prompt(x, g, wm, wf, bf, rw, layer, depth, kv_all):
    nb, t, _ = x.shape
    assert nb == SCAN_BB and len(rw) == N_RW_WEIGHTS
    tt = _row_tile(t, IN_TT)
    r = jnp.arange(nb * tt)
    perm = ((r[:, None] % nb) * tt + r[:, None] // nb == r[None, :]).astype(BF16)
    bm = lambda w: pl.BlockSpec((nb, tt, w), lambda i: (0, i, 0))
    tm = pl.BlockSpec((tt, nb, WIDTH), lambda i: (i, 0, 0))
    stacked = pl.BlockSpec((None, nb, tt, WIDTH), lambda i: (layer, 0, i, 0))
    sds = lambda w, dt: jax.ShapeDtypeStruct((nb, t, w), dt)
    all_sds = jax.ShapeDtypeStruct((depth, nb, t, WIDTH), F32)
    n_const = 6 + len(rw)
    return pl.pallas_call(
        functools.partial(_in_proj_prompt_kernel, n_alias=len(kv_all)),
        grid=(t // tt,),
        in_specs=[bm(D_MODEL), _const_spec((1, D_MODEL)), _const_spec(wm.shape),
                  _const_spec(wf.shape), _const_spec((1, LANES)), _const_spec(perm.shape)]
                 + [_const_spec(w.shape) for w in rw]
                 + [pl.BlockSpec(memory_space=pl.ANY) for _ in kv_all],
        out_specs=(bm(WIDTH), bm(WIDTH), stacked, stacked, bm(WIDTH), bm(WIDTH), bm(LANES),
                   pl.BlockSpec((nb, C_IN), lambda i: (0, 0))) + (tm,) * 8,
        out_shape=(sds(WIDTH, F32), sds(WIDTH, BF16), all_sds, all_sds,
                   sds(WIDTH, BF16), sds(WIDTH, BF16), sds(LANES, F32),
                   jax.ShapeDtypeStruct((nb, C_IN), F32))
                  + tuple(jax.ShapeDtypeStruct((t, nb, WIDTH), F32) for _ in range(8)),
        input_output_aliases={n_const + j: 2 + j for j in range(len(kv_all))},
        scratch_shapes=[pltpu.VMEM((nb, C_IN), F32)],
        compiler_params=_params(("arbitrary",)),
        name="in_proj_prompt",
    )(x, g, wm, wf, bf, perm, *rw, *kv_all)


def _in_proj(x, g, wm, wf, bf, q_dtype):
    n = x.shape[0]
    tm = _row_tile(n, 512)
    row = lambda w: pl.BlockSpec((tm, w), lambda i: (i, 0))
    out_shape = (
        jax.ShapeDtypeStruct((n, WIDTH), F32),
        jax.ShapeDtypeStruct((n, WIDTH), q_dtype),
        jax.ShapeDtypeStruct((n, WIDTH), F32),
        jax.ShapeDtypeStruct((n, WIDTH), F32),
        jax.ShapeDtypeStruct((n, WIDTH), BF16),
        jax.ShapeDtypeStruct((n, WIDTH), BF16),
        jax.ShapeDtypeStruct((n, C_IN), F32),
        jax.ShapeDtypeStruct((n, LANES), F32),
    )
    return pl.pallas_call(
        _in_proj_kernel,
        grid=(n // tm,),
        in_specs=[row(D_MODEL), _const_spec((1, D_MODEL)), _const_spec(wm.shape),
                  _const_spec(wf.shape), _const_spec((1, LANES))],
        out_specs=(row(WIDTH), row(WIDTH), row(WIDTH), row(WIDTH), row(WIDTH), row(WIDTH),
                   row(C_IN), row(LANES)),
        out_shape=out_shape,
        compiler_params=_params(("parallel",)),
        name="in_proj",
    )(x, g, wm, wf, bf)


CUM_BLK = 256


BIAS_TERMS = 3


def _cumsum_kernel(lf_ref, tri_ref, pq_ref, pk_ref, oq_ref, ok_ref, eq_ref, ek_ref):
    t = lf_ref.shape[1]
    carry = jnp.zeros((1, LANES), F32)
    for i in range(t // CUM_BLK):
        sl = slice(i * CUM_BLK, (i + 1) * CUM_BLK)
        c = _dot_right3(tri_ref[...], lf_ref[0, sl, :]) + carry
        terms = jnp.concatenate(_split3(c * LOG2E), axis=1)
        eq_ref[0, sl, :] = (_dot(terms, pq_ref[...]) + oq_ref[...]).astype(BF16)
        ek_ref[0, sl, :] = (ok_ref[...] - _dot(terms, pk_ref[...])).astype(BF16)
        carry = c[CUM_BLK - 1:CUM_BLK, :]


def _bias_lane_mats():
    src = jnp.arange(BIAS_TERMS * LANES)
    part, head = src // LANES, src % LANES
    dst = jnp.arange(NH * LANES)
    dhead, dlane = dst // LANES, dst % LANES
    other = HD * (1 - dhead % 2)
    slot = dlane - other
    same = head[:, None] == dhead[None, :]
    pq = (same & (slot[None, :] == part[:, None])).astype(BF16)
    pk = (same & (slot[None, :] == BIAS_TERMS + part[:, None])).astype(BF16)
    oq = ((slot >= BIAS_TERMS) & (slot < 2 * BIAS_TERMS)).astype(F32).reshape(1, -1)
    ok = ((slot >= 0) & (slot < BIAS_TERMS)).astype(F32).reshape(1, -1)
    return pq, pk, oq, ok


def _cumsum(lf, tri, mats):
    b, t, _ = lf.shape
    out = pl.BlockSpec((1, t, NH * LANES), lambda i: (i, 0, 0))
    return pl.pallas_call(
        _cumsum_kernel,
        grid=(b,),
        in_specs=[pl.BlockSpec((1, t, LANES), lambda i: (i, 0, 0)), _const_spec(tri.shape)]
                 + [_const_spec(m.shape) for m in mats],
        out_specs=(out, out),
        out_shape=(jax.ShapeDtypeStruct((b, t, NH * LANES), BF16),
                   jax.ShapeDtypeStruct((b, t, NH * LANES), BF16)),
        compiler_params=_params(("parallel",)),
        name="fox_cumsum",
    )(lf, tri, *mats)


def _flash_kernel(q_ref, k_ref, v_ref, eq_ref, ek_ref, o_ref, m_scr, l_scr, acc_scr, *, tq, tk):
    qi = pl.program_id(1)
    ki = pl.program_id(2)

    @pl.when(ki == 0)
    def _():
        m_scr[...] = jnp.full(m_scr.shape, NEG_INF, F32)
        l_scr[...] = jnp.zeros(l_scr.shape, F32)
        acc_scr[...] = jnp.zeros(acc_scr.shape, F32)

    def block(on_diagonal):
        lane = lax.broadcasted_iota(jnp.int32, (1, LANES), 1)
        own = [((lane // HD) == j).astype(F32).astype(BF16) for j in range(2)]
        if on_diagonal:
            causal = (lax.broadcasted_iota(jnp.int32, (tq, tk), 1)
                      <= lax.broadcasted_iota(jnp.int32, (tq, tk), 0))
        for hp in range(NHP):
            ls = slice(hp * LANES, (hp + 1) * LANES)
            q = q_ref[0, :, ls]
            k = k_ref[0, :, ls]
            v = v_ref[0, :, ls]
            pv = []
            alpha = []
            for j in range(2):
                h = 2 * hp + j
                hs = slice(h * LANES, (h + 1) * LANES)
                s = _dot_nt(q * own[j] + eq_ref[0, :, hs], k * own[j] + ek_ref[0, :, hs])
                if on_diagonal:
                    s = jnp.where(causal, s, NEG_INF)
                m_prev = m_scr[h]
                m_new = jnp.maximum(m_prev, jnp.max(s, axis=1, keepdims=True))
                a = jnp.exp2(m_prev - m_new)
                p = jnp.exp2(s - m_new[:, 0:1])
                l_scr[h] = a * l_scr[h] + jnp.sum(p, axis=1, keepdims=True)
                m_scr[h] = m_new
                pv.append(_dot(p.astype(BF16), v))
                alpha.append(a)
            first = (lane // HD) == 0
            acc_scr[hp] = (jnp.where(first, alpha[0], alpha[1]) * acc_scr[hp]
                           + jnp.where(first, pv[0], pv[1]))

    pl.when(ki < qi)(functools.partial(block, False))
    pl.when(ki == qi)(functools.partial(block, True))

    @pl.when(ki == qi)
    def _():
        lane = lax.broadcasted_iota(jnp.int32, (1, LANES), 1)
        for hp in range(NHP):
            l = jnp.where((lane // HD) == 0, l_scr[2 * hp], l_scr[2 * hp + 1])
            o_ref[0, :, hp * LANES:(hp + 1) * LANES] = (acc_scr[hp] / l).astype(o_ref.dtype)


def _flash(q, k, v, eq, ek, *, tq):
    b, t, _ = q.shape
    tq = min(tq, t)
    nq = t // tq
    kv_idx = lambda bi, qi, ki: (bi, jnp.minimum(ki, qi), 0)
    return pl.pallas_call(
        functools.partial(_flash_kernel, tq=tq, tk=tq),
        grid=(b, nq, nq),
        in_specs=[pl.BlockSpec((1, tq, WIDTH), lambda bi, qi, ki: (bi, qi, 0)),
                  pl.BlockSpec((1, tq, WIDTH), kv_idx),
                  pl.BlockSpec((1, tq, WIDTH), kv_idx),
                  pl.BlockSpec((1, tq, NH * LANES), lambda bi, qi, ki: (bi, qi, 0)),
                  pl.BlockSpec((1, tq, NH * LANES), kv_idx)],
        out_specs=pl.BlockSpec((1, tq, WIDTH), lambda bi, qi, ki: (bi, qi, 0)),
        out_shape=jax.ShapeDtypeStruct((b, t, WIDTH), BF16),
        scratch_shapes=[pltpu.VMEM((NH, tq, LANES), F32), pltpu.VMEM((NH, tq, LANES), F32),
                        pltpu.VMEM((NHP, tq, LANES), F32)],
        compiler_params=_params(("parallel", "parallel", "arbitrary")),
        name="fox_flash",
    )(q, k, v, eq, ek)


def _decode_bias_kernel(pt_ref, tbl_ref, lfn_ref, msuf_ref, mtot_ref, plater_ref, o_ref, g_scr):
    b = pl.program_id(0)
    npages = g_scr.shape[0]
    for p in range(npages):
        g_scr[p] = tbl_ref[pt_ref[b, p]]
    g = g_scr[...].reshape(npages * NH, PAGE_SIZE)
    within = _dot_left3(g, msuf_ref[...])
    tot = _dot_left3(g, mtot_ref[...])
    later = _dot_right3(plater_ref[...], tot)
    o_ref[0] = (within + later).reshape(npages, NH, PAGE_SIZE) + lfn_ref[...]


def _decode_bias(page_table, tbl, lfn, msuf, mtot, plater):
    nb, npages = page_table.shape
    grid_spec = pltpu.PrefetchScalarGridSpec(
        num_scalar_prefetch=1,
        grid=(nb,),
        in_specs=[_const_spec(tbl.shape),
                  pl.BlockSpec((1, NH, PAGE_SIZE), lambda b, pt: (b, 0, 0)),
                  _const_spec(msuf.shape), _const_spec(mtot.shape), _const_spec(plater.shape)],
        out_specs=pl.BlockSpec((1, npages, NH, PAGE_SIZE), lambda b, pt: (b, 0, 0, 0)),
        scratch_shapes=[pltpu.VMEM((npages, NH, PAGE_SIZE), F32)],
    )
    return pl.pallas_call(
        _decode_bias_kernel,
        grid_spec=grid_spec,
        out_shape=jax.ShapeDtypeStruct((nb, npages, NH, PAGE_SIZE), F32),
        compiler_params=_params(("arbitrary",)),
        name="fox_decode_bias",
    )(page_table, tbl, lfn, msuf, mtot, plater)


PAGES_PER_STEP = 16


def _col_bcast(row_pair, eye2, j):
    lane = lax.broadcasted_iota(jnp.int32, (1, LANES), 1)
    diag = jnp.where((lane // HD) == j, jnp.broadcast_to(row_pair, (HD, LANES)) * eye2, 0.0)
    return jnp.broadcast_to(jnp.sum(diag, axis=1, keepdims=True), (HD, LANES))


def _decode_attn_kernel(pt_ref, q_ref, kn_ref, vn_ref, bias_ref, eye_ref, *refs):
    k_refs = refs[0:PAGES_PER_STEP]
    v_refs = refs[PAGES_PER_STEP:2 * PAGES_PER_STEP]
    o_ref, m_scr, l_scr, acc_scr, qc_scr = refs[2 * PAGES_PER_STEP:]
    g = pl.program_id(1)
    eye2 = eye_ref[...]
    lane = lax.broadcasted_iota(jnp.int32, (1, LANES), 1)
    sub = lax.broadcasted_iota(jnp.int32, (NH, PAGE_SIZE), 0)

    @pl.when(g == 0)
    def _():
        q = q_ref[0]
        own = (lax.broadcasted_iota(jnp.int32, (NH, WIDTH), 1) // HD
               == lax.broadcasted_iota(jnp.int32, (NH, WIDTH), 0))
        qk = jnp.broadcast_to(q * kn_ref[0], (NH, WIDTH))
        s_new = jnp.sum(jnp.where(own, qk, 0.0), axis=1, keepdims=True)
        m_scr[...] = jnp.broadcast_to(s_new, m_scr.shape)
        l_scr[...] = jnp.ones(l_scr.shape, F32)
        vn = vn_ref[0]
        for h in range(NH):
            ls = slice((h // 2) * LANES, (h // 2 + 1) * LANES)
            qc_scr[h] = _col_bcast(q[:, ls], eye2, h % 2)
            acc_scr[h] = jnp.where(lane == 0, _col_bcast(vn[:, ls], eye2, h % 2), 0.0)

    s_parts = []
    for j in range(PAGES_PER_STEP):
        s_page = jnp.zeros((NH, PAGE_SIZE), F32)
        for h in range(NH):
            sh = jnp.sum(k_refs[j][0, 0, h] * qc_scr[h], axis=0, keepdims=True)
            s_page = jnp.where(sub == h, jnp.broadcast_to(sh, (NH, PAGE_SIZE)), s_page)
        s_parts.append(s_page + bias_ref[0, j])
    s = jnp.concatenate(s_parts, axis=1)
    m_prev = m_scr[...]
    m_new = jnp.maximum(m_prev, jnp.max(s, axis=1, keepdims=True))
    a = jnp.exp(m_prev - m_new)
    p = jnp.exp(s - m_new[:, 0:1])
    l_scr[...] = a * l_scr[...] + jnp.sum(p, axis=1, keepdims=True)
    m_scr[...] = m_new
    for h in range(NH):
        acc = acc_scr[h] * a[h:h + 1, 0:1]
        for j in range(PAGES_PER_STEP):
            ph = p[h:h + 1, j * PAGE_SIZE:(j + 1) * PAGE_SIZE]
            acc = acc + v_refs[j][0, 0, h] * jnp.broadcast_to(ph, (HD, PAGE_SIZE))
        acc_scr[h] = acc

    @pl.when(g == pl.num_programs(1) - 1)
    def _():
        rows = []
        for hp in range(NHP):
            pair = []
            for j in range(2):
                h = 2 * hp + j
                col = jnp.sum(acc_scr[h], axis=1, keepdims=True) / l_scr[h:h + 1, 0:1]
                pair.append(jnp.sum(jnp.broadcast_to(col, (HD, LANES)) * eye2, axis=0,
                                    keepdims=True))
            rows.append(jnp.where((lane // HD) == 0, pair[0], pair[1]))
        o_ref[0] = jnp.concatenate(rows, axis=1)


def _decode_attn(page_table, q, k_new, v_new, bias, eye2, cache_kt, cache_vt, layer):
    nb, npages = page_table.shape
    ng = npages // PAGES_PER_STEP

    def page_spec(j):
        return pl.BlockSpec((1, 1, NH, HD, PAGE_SIZE),
                            lambda b, g, pt: (layer, pt[b, g * PAGES_PER_STEP + j], 0, 0, 0))

    vec = lambda: pl.BlockSpec((1, 1, WIDTH), lambda b, g, pt: (b, 0, 0))
    grid_spec = pltpu.PrefetchScalarGridSpec(
        num_scalar_prefetch=1,
        grid=(nb, ng),
        in_specs=[vec(), vec(), vec(),
                  pl.BlockSpec((1, PAGES_PER_STEP, NH, PAGE_SIZE), lambda b, g, pt: (b, g, 0, 0)),
                  _const_spec(eye2.shape)]
                 + [page_spec(j) for j in range(PAGES_PER_STEP)]
                 + [page_spec(j) for j in range(PAGES_PER_STEP)],
        out_specs=pl.BlockSpec((1, 1, WIDTH), lambda b, g, pt: (b, 0, 0)),
        scratch_shapes=[pltpu.VMEM((NH, LANES), F32), pltpu.VMEM((NH, LANES), F32),
                        pltpu.VMEM((NH, HD, PAGE_SIZE), F32), pltpu.VMEM((NH, HD, LANES), F32)],
    )
    return pl.pallas_call(
        _decode_attn_kernel,
        grid_spec=grid_spec,
        out_shape=jax.ShapeDtypeStruct((nb, 1, WIDTH), F32),
        compiler_params=_params(("parallel", "arbitrary")),
        name="fox_decode_attn",
    )(page_table, q, k_new, v_new, bias, eye2,
      *([cache_kt] * PAGES_PER_STEP), *([cache_vt] * PAGES_PER_STEP))


POOL_HALO = 16


def _pool_project(pooled, wp_ref, sc_ref):
    outs = []
    for g in range(len(POOL_WINDOWS)):
        ls = slice(g * GROUP_W, (g + 1) * GROUP_W)
        outs.append(_dot(pooled[g].astype(BF16), wp_ref[g]) * sc_ref[:, ls])
    return outs


def _pool_prompt_kernel(u_ref, wp_ref, sc_ref, o_ref, ext_scr, *, tt):
    ti = pl.program_id(1)

    @pl.when(ti == 0)
    def _():
        ext_scr[0:POOL_HALO, :] = jnp.zeros((POOL_HALO, A_WIDTH), F32)

    ext_scr[POOL_HALO:POOL_HALO + tt, :] = u_ref[0]
    pos1 = ti * tt + lax.broadcasted_iota(jnp.int32, (tt, GROUP_W), 0) + 1
    pooled = []
    for g, w in enumerate(POOL_WINDOWS):
        ls = slice(g * GROUP_W, (g + 1) * GROUP_W)
        x = ext_scr[POOL_HALO:POOL_HALO + tt, ls]
        win = x
        for d in range(1, w):
            win = win + ext_scr[POOL_HALO - d:POOL_HALO - d + tt, ls]
        cnt = jnp.minimum(w, pos1).astype(F32)
        pooled.append(win / cnt - x)
    outs = _pool_project(pooled, wp_ref, sc_ref)
    for g in range(len(POOL_WINDOWS)):
        o_ref[0, :, g * GROUP_W:(g + 1) * GROUP_W] = outs[g].astype(o_ref.dtype)
    ext_scr[0:POOL_HALO, :] = ext_scr[tt:tt + POOL_HALO, :]


def _pool_prompt(u, wp, sc):
    b, t, _ = u.shape
    tt = _row_tile(t, 512)
    return pl.pallas_call(
        functools.partial(_pool_prompt_kernel, tt=tt),
        grid=(b, t // tt),
        in_specs=[pl.BlockSpec((1, tt, A_WIDTH), lambda bi, ti: (bi, ti, 0)),
                  _const_spec(wp.shape), _const_spec((1, A_WIDTH))],
        out_specs=pl.BlockSpec((1, tt, A_WIDTH), lambda bi, ti: (bi, ti, 0)),
        out_shape=jax.ShapeDtypeStruct((b, t, A_WIDTH), BF16),
        scratch_shapes=[pltpu.VMEM((POOL_HALO + tt, A_WIDTH), F32)],
        compiler_params=_params(("parallel", "arbitrary")),
        name="pool_prompt",
    )(u, wp, sc)


def _pool_sample_kernel(u_ref, prev_ref, wp_ref, sc_ref, o_ref):
    pooled = []
    for g, w in enumerate(POOL_WINDOWS):
        ls = slice(g * GROUP_W, (g + 1) * GROUP_W)
        x = u_ref[:, ls]
        win = x
        for d in range(1, w):
            win = win + prev_ref[POOL_PREFIX - d, :, ls]
        pooled.append(win / float(w) - x)
    outs = _pool_project(pooled, wp_ref, sc_ref)
    for g in range(len(POOL_WINDOWS)):
        o_ref[:, g * GROUP_W:(g + 1) * GROUP_W] = outs[g]


def _pool_sample(u, prev_t, wp, sc):
    nb = u.shape[0]
    return pl.pallas_call(
        _pool_sample_kernel,
        out_shape=jax.ShapeDtypeStruct((nb, A_WIDTH), F32),
        compiler_params=pltpu.CompilerParams(vmem_limit_bytes=VMEM_LIMIT),
        name="pool_sample",
    )(u, prev_t, wp, sc)


def _rwkv_prep_math(pc, prev, mu_ref, wlo_ref, w0_ref, a0_ref, gup_ref, kk_ref, ka_ref, rk_ref,
                    seg_ref):
    xs = pc + (prev - pc) * mu_ref[...]
    r = xs[:, 0:WIDTH]
    k = xs[:, WIDTH:2 * WIDTH]
    v = xs[:, 2 * WIDTH:3 * WIDTH]
    lora = xs[:, 3 * WIDTH:3 * WIDTH + LANES]
    gd = xs[:, 3 * WIDTH + LANES:3 * WIDTH + 2 * LANES]
    lane = lax.broadcasted_iota(jnp.int32, (1, LANES), 1)
    lhs = jnp.where(lane < DECAY_LORA, jnp.tanh(lora), lora).astype(BF16)
    up = _dot(lhs, wlo_ref[...])
    w = -_softplus(-(w0_ref[...] + up[:, 0:WIDTH])) - 0.5
    decay = jnp.exp(-jnp.exp(w))
    a = _sigmoid(a0_ref[...] + up[:, WIDTH:2 * WIDTH])
    g = _dot(_sigmoid(gd).astype(BF16), gup_ref[...])
    kk = k * kk_ref[...]
    nrm = jnp.sqrt(_dot_left2(kk * kk, seg_ref[...]))
    kk = kk / jnp.maximum(nrm, 1e-12)
    k2 = k * (1.0 + (a - 1.0) * ka_ref[...])
    bonus = _dot_left2(r * k2 * rk_ref[...], seg_ref[...]) * v
    return r, decay, k2, v, -kk, kk * a, g, bonus


def _rwkv_prep_sample_kernel(pc_ref, prev_ref, mu_ref, wlo_ref, w0_ref, a0_ref, gup_ref, kk_ref,
                             ka_ref, rk_ref, seg_ref, *out_refs):
    outs = _rwkv_prep_math(pc_ref[...], prev_ref[...], mu_ref, wlo_ref, w0_ref, a0_ref, gup_ref,
                           kk_ref, ka_ref, rk_ref, seg_ref)
    for o_ref, val in zip(out_refs, outs):
        o_ref[...] = val


def _rwkv_prep_sample(pc, prev, weights):
    nb = pc.shape[0]
    return pl.pallas_call(
        _rwkv_prep_sample_kernel,
        out_shape=tuple(jax.ShapeDtypeStruct((nb, WIDTH), F32) for _ in range(8)),
        compiler_params=pltpu.CompilerParams(vmem_limit_bytes=VMEM_LIMIT),
        name="rwkv_prep_sample",
    )(pc, prev, *weights)


def _scan_kernel(r_ref, w_ref, k_ref, v_ref, kn_ref, b_ref, s0_ref, seg_ref, eye_ref,
                 y_ref, so_ref, s_scr, *, tc, zero_init):
    ci = pl.program_id(1)

    @pl.when(ci == 0)
    def _():
        if zero_init:
            s_scr[...] = jnp.zeros(s_scr.shape, F32)
        else:
            for b in range(SCAN_BB):
                for hp in range(NHP):
                    s_scr[b, hp] = jnp.concatenate([s0_ref[b, 2 * hp], s0_ref[b, 2 * hp + 1]],
                                                   axis=1)

    seg = seg_ref[...]
    eye2 = eye_ref[...]
    groups = [(b, hp) for b in range(SCAN_BB) for hp in range(NHP)]
    ng = len(groups)
    sub = lax.broadcasted_iota(jnp.int32, (SCAN_BB, LANES), 0)

    def row(slab, b, hp):
        return jnp.broadcast_to(slab[b:b + 1, hp * LANES:(hp + 1) * LANES], (HD, LANES))

    def row16(slab, b, hp):
        return row(slab, b, hp).astype(BF16)

    def seg_sum(parts):
        return _dot(jnp.concatenate(parts, axis=0), seg)

    eye16 = eye2.astype(BF16)

    def write_y(yb, t):
        y_tiles = [jnp.zeros((SCAN_BB, LANES), F32) for _ in range(NHP)]
        for n, (b, hp) in enumerate(groups):
            y_row = jnp.sum(yb[n * HD:(n + 1) * HD] * eye2, axis=0, keepdims=True)
            y_tiles[hp] = jnp.where(sub == b, jnp.broadcast_to(y_row, (SCAN_BB, LANES)),
                                    y_tiles[hp])
        y_ref[t] = jnp.concatenate(y_tiles, axis=1)

    def step(t, carry):
        tp = jnp.maximum(t - 1, 0)
        w_t, k_t, v_t, kn_t, b_t = (ref[t] for ref in (w_ref, k_ref, v_ref, kn_ref, b_ref))
        r_p = r_ref[tp]
        s16 = [s_scr[b, hp].astype(BF16) for b, hp in groups]
        res = seg_sum([s16[n] * row16(kn_t, b, hp) for n, (b, hp) in enumerate(groups)]
                      + [eye16 * row16(v_t, b, hp) for b, hp in groups]
                      + [s16[n] * row16(r_p, b, hp) for n, (b, hp) in enumerate(groups)])
        for n, (b, hp) in enumerate(groups):
            sa = res[n * HD:(n + 1) * HD]
            vb = res[(ng + n) * HD:(ng + n + 1) * HD]
            s_scr[b, hp] = (s_scr[b, hp] * row(w_t, b, hp) + sa * row(b_t, b, hp)
                            + vb * row(k_t, b, hp))
        write_y(res[2 * ng * HD:], tp)
        return carry

    lax.fori_loop(0, tc, step, 0)
    r_last = r_ref[tc - 1]
    write_y(seg_sum([s_scr[b, hp].astype(BF16) * row16(r_last, b, hp) for b, hp in groups]),
            tc - 1)

    @pl.when(ci == pl.num_programs(1) - 1)
    def _():
        for b in range(SCAN_BB):
            for hp in range(NHP):
                tile = s_scr[b, hp]
                so_ref[b, 2 * hp] = tile[:, 0:HD]
                so_ref[b, 2 * hp + 1] = tile[:, HD:2 * HD]


def _scan(r, w, k, v, kn, bv, s0, seg2, eye2, *, tc):
    t, nb, _ = r.shape
    tc = min(tc, t)
    zero_init = s0 is None
    if zero_init:
        s0 = jnp.zeros((SCAN_BB, NH, HD, HD), F32)
        s0_spec = pl.BlockSpec((SCAN_BB, NH, HD, HD), lambda bi, ci: (0, 0, 0, 0))
    else:
        s0_spec = pl.BlockSpec((SCAN_BB, NH, HD, HD), lambda bi, ci: (bi, 0, 0, 0))
    seq = lambda: pl.BlockSpec((tc, SCAN_BB, WIDTH), lambda bi, ci: (ci, bi, 0))
    return pl.pallas_call(
        functools.partial(_scan_kernel, tc=tc, zero_init=zero_init),
        grid=(nb // SCAN_BB, t // tc),
        in_specs=[seq(), seq(), seq(), seq(), seq(), seq(), s0_spec,
                  _const_spec(seg2.shape), _const_spec(eye2.shape)],
        out_specs=(seq(), pl.BlockSpec((SCAN_BB, NH, HD, HD), lambda bi, ci: (bi, 0, 0, 0))),
        out_shape=(jax.ShapeDtypeStruct((t, nb, WIDTH), F32),
                   jax.ShapeDtypeStruct((nb, NH, HD, HD), F32)),
        scratch_shapes=[pltpu.VMEM((SCAN_BB, NHP, HD, LANES), F32)],
        compiler_params=_params(("parallel", "arbitrary")),
        name="rwkv_scan",
    )(r, w, k, v, kn, bv, s0, seg2, eye2)


def _rwkv_post_kernel(y_ref, bonus_ref, g_ref, lnw_ref, lnb_ref, seg_ref, o_ref):
    y = y_ref[...]
    seg = seg_ref[...]
    mean = _dot_left2(y, seg) * (1.0 / HD)
    d = y - mean
    var = _dot_left2(d * d, seg) * (1.0 / HD)
    yn = d * lax.rsqrt(var + LN_X_EPS) * lnw_ref[...] + lnb_ref[...]
    o_ref[...] = ((yn + bonus_ref[...]) * g_ref[...]).astype(o_ref.dtype)


def _rwkv_post(y, bonus, g, lnw, lnb, seg, out_dtype):
    n = y.shape[0]
    tm = _row_tile(n, 512)
    row = pl.BlockSpec((tm, WIDTH), lambda i: (i, 0))
    return pl.pallas_call(
        _rwkv_post_kernel,
        grid=(n // tm,),
        in_specs=[row, row, row, _const_spec((1, WIDTH)), _const_spec((1, WIDTH)),
                  _const_spec(seg.shape)],
        out_specs=row,
        out_shape=jax.ShapeDtypeStruct((n, WIDTH), out_dtype),
        compiler_params=_params(("parallel",)),
        name="rwkv_post",
    )(y, bonus, g, lnw, lnb, seg)


def _merge_kernel(x_ref, ya_ref, yb_ref, yc_ref, gpre_ref, gpost_ref, wg_ref, wa_ref, wb_ref,
                  wc_ref, wo_ref, o_ref):
    x = x_ref[...]
    h = _rms(x, gpre_ref[...]).astype(BF16)
    merged = None
    for n, (y_ref, w_ref) in enumerate(((ya_ref, wa_ref), (yb_ref, wb_ref), (yc_ref, wc_ref))):
        gate = _sigmoid(_dot(h, wg_ref[:, n * D_MODEL:(n + 1) * D_MODEL]))
        term = gate * _dot(y_ref[...].astype(BF16), w_ref[...])
        merged = term if merged is None else merged + term
    o = _dot(merged.astype(BF16), wo_ref[...])
    o_ref[...] = x + _rms(o, gpost_ref[...])


def _merge(x, ya, yb, yc, gpre, gpost, wg, wa, wb, wc, wo):
    n = x.shape[0]
    tm = _row_tile(n, 512)
    row = lambda w: pl.BlockSpec((tm, w), lambda i: (i, 0))
    return pl.pallas_call(
        _merge_kernel,
        grid=(n // tm,),
        in_specs=[row(D_MODEL), row(WIDTH), row(WIDTH), row(WIDTH),
                  _const_spec((1, D_MODEL)), _const_spec((1, D_MODEL)),
                  _const_spec(wg.shape), _const_spec(wa.shape), _const_spec(wb.shape),
                  _const_spec(wc.shape), _const_spec(wo.shape)],
        out_specs=row(D_MODEL),
        out_shape=jax.ShapeDtypeStruct((n, D_MODEL), F32),
        compiler_params=_params(("parallel",)),
        name="merge",
    )(x, ya, yb, yc, gpre, gpost, wg, wa, wb, wc, wo)


def _mlp_kernel(x_ref, gpre_ref, gpost_ref, wu_ref, wd_ref, o_ref):
    x = x_ref[...]
    h = _rms(x, gpre_ref[...]).astype(BF16)
    up = jnp.maximum(_dot(h, wu_ref[...]), 0.0)
    down = _dot((up * up).astype(BF16), wd_ref[...])
    o_ref[...] = x + _rms(down, gpost_ref[...])


def _mlp(x, gpre, gpost, wu, wd):
    n = x.shape[0]
    tm = _row_tile(n, 512)
    row = pl.BlockSpec((tm, D_MODEL), lambda i: (i, 0))
    return pl.pallas_call(
        _mlp_kernel,
        grid=(n // tm,),
        in_specs=[row, _const_spec((1, D_MODEL)), _const_spec((1, D_MODEL)),
                  _const_spec(wu.shape), _const_spec(wd.shape)],
        out_specs=row,
        out_shape=jax.ShapeDtypeStruct((n, D_MODEL), F32),
        compiler_params=_params(("parallel",)),
        name="mlp",
    )(x, gpre, gpost, wu, wd)


def _seg_ones(n):
    i = jnp.arange(n)
    return (i[:, None] // HD == i[None, :] // HD).astype(BF16)


def _decode_bias_mats(npages):
    pos = jnp.arange(PAGE_SIZE)
    msuf = (pos[:, None] > pos[None, :]).astype(BF16)
    mtot = jnp.ones((PAGE_SIZE, PAGE_SIZE), BF16)
    r = jnp.arange(npages * NH)
    plater = ((r[None, :] // NH > r[:, None] // NH)
              & (r[None, :] % NH == r[:, None] % NH)).astype(BF16)
    return msuf, mtot, plater


def _layer_weights(l, p):
    w_in = p["w_in"][l]
    row = lambda a: a[l].reshape(1, -1)
    wlo = jnp.zeros((LANES, 2 * WIDTH), F32)
    wlo = wlo.at[0:DECAY_LORA, 0:WIDTH].set(p["rwkv_w_up"][l])
    wlo = wlo.at[DECAY_LORA:, WIDTH:].set(p["rwkv_a_up"][l])
    return dict(
        g_mix_pre=row(p["norm_mix_pre"]), g_mix_post=row(p["norm_mix_post"]),
        g_mlp_pre=row(p["norm_mlp_pre"]), g_mlp_post=row(p["norm_mlp_post"]),
        wm=jnp.concatenate([w_in[:, 0:OFF_F], w_in[:, OFF_C:OFF_G]], axis=1).astype(BF16),
        wf=jnp.pad(w_in[:, OFF_F:OFF_C], ((0, 0), (0, LANES - NH))).astype(BF16),
        bf=jnp.pad(p["b_forget"][l], (0, LANES - NH)).reshape(1, LANES),
        wg=w_in[:, OFF_G:].astype(BF16),
        wp=p["w_pool"][l].astype(BF16), pool_scale=row(p["pool_scale"]),
        rw=(row(p["rwkv_mu"]), wlo.astype(BF16), row(p["rwkv_w0"]), row(p["rwkv_a0"]),
            p["rwkv_g_up"][l].astype(BF16), row(p["rwkv_k_k"]), row(p["rwkv_k_a"]),
            row(p["rwkv_r_k"])),
        lnw=row(p["rwkv_ln_w"]), lnb=row(p["rwkv_ln_b"]),
        wa=p["w_branch_a"][l].astype(BF16), wb=p["w_branch_b"][l].astype(BF16),
        wc=p["w_branch_c"][l].astype(BF16), wo=p["w_out"][l].astype(BF16),
        wu=p["w_mlp_up"][l].astype(BF16), wd=p["w_mlp_down"][l].astype(BF16),
    )


def _prompt_layer(x, lw, consts, layer, depth, kv_all):
    b, t, _ = x.shape
    n = b * t
    x2 = x.reshape(n, D_MODEL)
    (u, q, k_all, v_all, kb, vb, lf, pc_last, r, w, k2, vv, kn, bv, g, bonus) = _in_proj_prompt(
        x, lw["g_mix_pre"], lw["wm"], lw["wf"], lw["bf"], lw["rw"] + (consts["seg2"],),
        layer, depth, kv_all)
    ya = _pool_prompt(u, lw["wp"], lw["pool_scale"])
    eq, ek = _cumsum(lf, consts["tri"], consts["bias_lanes"])
    yb = _flash(q, kb, vb, eq, ek, tq=512)
    y, s_fin = _scan(r, w, k2, vv, kn, bv, None, consts["seg2"], consts["eye2"], tc=128)
    yc = _rwkv_post(y.reshape(n, WIDTH), bonus.reshape(n, WIDTH), g.reshape(n, WIDTH),
                    lw["lnw"], lw["lnb"], consts["seg2"], BF16)
    yc = jnp.transpose(yc.reshape(t, b, WIDTH), (1, 0, 2)).reshape(n, WIDTH)
    x2 = _merge(x2, ya.reshape(n, WIDTH), yb.reshape(n, WIDTH), yc, lw["g_mix_pre"],
                lw["g_mix_post"], lw["wg"], lw["wa"], lw["wb"], lw["wc"], lw["wo"])
    x2 = _mlp(x2, lw["g_mlp_pre"], lw["g_mlp_post"], lw["wu"], lw["wd"])
    new = (lf[:, :, 0:NH], u[:, t - POOL_PREFIX:], pc_last, s_fin)
    return x2.reshape(b, t, D_MODEL), new, (k_all, v_all)


def _sample_layer(x, lw, consts, l, cache_k, cache_v, cache_lf, state_pool, state_shift,
                  state_wkv, page_table):
    nb = x.shape[0]
    x2 = x.reshape(nb, D_MODEL)
    u, q, k, v, _, _, pc, lf = _in_proj(x2, lw["g_mix_pre"], lw["wm"], lw["wf"], lw["bf"], F32)
    ya = _pool_sample(u, jnp.transpose(state_pool, (1, 0, 2)), lw["wp"], lw["pool_scale"])
    lfn = jnp.broadcast_to(lf[:, 0:NH, None], (nb, NH, PAGE_SIZE))
    bias = _decode_bias(page_table, cache_lf, lfn, *consts["bias_mats"])
    yb = _decode_attn(page_table, q.reshape(nb, 1, WIDTH), k.reshape(nb, 1, WIDTH),
                      v.reshape(nb, 1, WIDTH), bias, consts["eye2"], cache_k, cache_v, l)
    rw = lw["rw"] + (consts["seg2"],)
    r, w, k2, vv, kn, bv, g, bonus = _rwkv_prep_sample(pc, state_shift, rw)
    e3 = lambda a: a.reshape(1, nb, WIDTH)
    y, s_fin = _scan(e3(r), e3(w), e3(k2), e3(vv), e3(kn), e3(bv), state_wkv,
                     consts["seg2"], consts["eye2"], tc=1)
    yc = _rwkv_post(y.reshape(nb, WIDTH), bonus, g, lw["lnw"], lw["lnb"], consts["seg2"], F32)
    x2 = _merge(x2, ya, yb.reshape(nb, WIDTH), yc, lw["g_mix_pre"], lw["g_mix_post"], lw["wg"],
                lw["wa"], lw["wb"], lw["wc"], lw["wo"])
    x2 = _mlp(x2, lw["g_mlp_pre"], lw["g_mlp_post"], lw["wu"], lw["wd"])
    new = (k.reshape(nb, 1, NH, HD), v.reshape(nb, 1, NH, HD), lf[:, 0:NH].reshape(nb, 1, NH),
           jnp.concatenate([state_pool[:, 1:], u[:, None, :]], axis=1), pc,
           s_fin)
    return x2.reshape(nb, 1, D_MODEL), new


def kernel(x_prompt, x_sample, cache_k, cache_v, cache_logf, state_pool, state_shift, state_wkv, page_table, norm_mix_pre, norm_mix_post, norm_mlp_pre, norm_mlp_post, w_in, b_forget, w_pool, pool_scale, rwkv_mu, rwkv_w0, rwkv_w_up, rwkv_a0, rwkv_a_up, rwkv_g_up, rwkv_k_k, rwkv_k_a, rwkv_r_k, rwkv_ln_w, rwkv_ln_b, w_branch_a, w_branch_b, w_branch_c, w_out, w_mlp_up, w_mlp_down):
    p = dict(norm_mix_pre=norm_mix_pre, norm_mix_post=norm_mix_post, norm_mlp_pre=norm_mlp_pre,
             norm_mlp_post=norm_mlp_post, w_in=w_in, b_forget=b_forget, w_pool=w_pool,
             pool_scale=pool_scale, rwkv_mu=rwkv_mu, rwkv_w0=rwkv_w0, rwkv_w_up=rwkv_w_up,
             rwkv_a0=rwkv_a0, rwkv_a_up=rwkv_a_up, rwkv_g_up=rwkv_g_up, rwkv_k_k=rwkv_k_k,
             rwkv_k_a=rwkv_k_a, rwkv_r_k=rwkv_r_k, rwkv_ln_w=rwkv_ln_w, rwkv_ln_b=rwkv_ln_b,
             w_branch_a=w_branch_a, w_branch_b=w_branch_b, w_branch_c=w_branch_c, w_out=w_out,
             w_mlp_up=w_mlp_up, w_mlp_down=w_mlp_down)
    depth, pool_pages = cache_k.shape[0], cache_k.shape[1]
    npages = page_table.shape[1]
    i = jnp.arange(CUM_BLK)
    eye = (jnp.arange(HD)[:, None] == (jnp.arange(LANES) % HD)[None, :]).astype(F32)
    consts = dict(seg2=_seg_ones(LANES), eye2=eye,
                  tri=(i[:, None] >= i[None, :]).astype(BF16),
                  bias_mats=_decode_bias_mats(npages), bias_lanes=_bias_lane_mats())
    ck = jnp.transpose(cache_k, (0, 1, 3, 4, 2))
    cv = jnp.transpose(cache_v, (0, 1, 3, 4, 2))
    clf = jnp.transpose(cache_logf, (0, 1, 3, 2))
    yp, ys = x_prompt, x_sample
    outs_p, outs_s = [], []
    kv_all = ()
    for l in range(depth):
        lw = _layer_weights(l, p)
        yp, new_p, kv_all = _prompt_layer(yp, lw, consts, l, depth, kv_all)
        outs_p.append(new_p)
        ys, new_s = _sample_layer(ys, lw, consts, l, ck, cv, clf[l], state_pool[l],
                                  state_shift[l], state_wkv[l], page_table)
        outs_s.append(new_s)
    stack = lambda outs: tuple(jnp.stack([o[j] for o in outs]) for j in range(len(outs[0])))
    bp, tp = x_prompt.shape[0], x_prompt.shape[1]
    p_kv = tuple(a.reshape(depth, bp, tp, NH, HD) for a in kv_all)
    return (yp, ys) + p_kv + stack(outs_p) + stack(outs_s)
```
